```python
import math
import jax, jax.numpy as jnp
from jax import lax
import numpy as np

D_MODEL = 4096
BATCH = 2
SEQ = 4096
DEPTH = 1

MEM_LEN = 256
RET_HEADS = 8
RET_DK = 128
RET_DV = 256
RET_CHUNK = 128
ATT_HEADS = 8
ATT_DH = 128
IDX_HEADS = 16
IDX_DH = 64
TOPK_MAX = 256
Q_BLOCK = 128
MEM_HEADS = 4
MEM_DH = 256
REL_BUCKETS = 32
REL_MAX_DIST = 128
PEER_HEADS = 8
PEER_NKEYS = 128
PEER_NEXPERTS = PEER_NKEYS * PEER_NKEYS
PEER_DQ = 256
PEER_TOPK = 16
PEER_BLOCK = 64
N_BRANCHES = 3
EPS = 1e-6
ROPE_BASE = 10000.0

RET_QK_W = RET_HEADS * RET_DK
RET_V_W = RET_HEADS * RET_DV
ATT_Q_W = ATT_HEADS * ATT_DH
IDX_Q_W = IDX_HEADS * IDX_DH
MEM_Q_W = MEM_HEADS * MEM_DH
GATE_W = N_BRANCHES * D_MODEL
SPLIT_SIZES = (RET_QK_W, RET_QK_W, RET_V_W, RET_V_W, ATT_Q_W, ATT_DH, ATT_DH, IDX_Q_W, IDX_DH, IDX_HEADS, MEM_Q_W, GATE_W)
IN_WIDTH = sum(SPLIT_SIZES)
SPLIT_POINTS = tuple(int(v) for v in np.cumsum(SPLIT_SIZES)[:-1])

kernel_name = "hybrid_retention_dsa_memory_peer_block"


def _rms(x):
    xf = x.astype(jnp.float32)
    return xf * lax.rsqrt(jnp.mean(xf * xf, axis=-1, keepdims=True) + EPS)


def rms_norm(x, g):
    return (_rms(x) * g.astype(jnp.float32)).astype(x.dtype)


def rotary(x, pos):
    half = x.shape[-1] // 2
    inv_freq = 1.0 / (ROPE_BASE ** (jnp.arange(half, dtype=jnp.float32) / half))
    ang = pos.astype(jnp.float32)[:, :, None] * inv_freq
    cos = jnp.cos(ang)[:, :, None, :]
    sin = jnp.sin(ang)[:, :, None, :]
    xf = x.astype(jnp.float32)
    x1, x2 = xf[..., :half], xf[..., half:]
    return jnp.concatenate([x1 * cos - x2 * sin, x1 * sin + x2 * cos], axis=-1)


def retention(q, k, v):
    B, S, H, dk = q.shape
    dv = v.shape[-1]
    C = RET_CHUNK
    n = S // C
    log_gamma = jnp.log1p(-jnp.exp2(-5.0 - jnp.arange(H, dtype=jnp.float32)))
    idx = jnp.arange(C, dtype=jnp.float32)
    diff = idx[:, None] - idx[None, :]
    inner_decay = jnp.where(diff[None] >= 0, jnp.exp(log_gamma[:, None, None] * jnp.maximum(diff, 0.0)[None]), 0.0)
    q_decay = jnp.exp(log_gamma[:, None] * (idx + 1.0)[None])
    k_decay = jnp.exp(log_gamma[:, None] * (C - 1.0 - idx)[None])
    chunk_decay = jnp.exp(log_gamma * C)
    qf = q.astype(jnp.float32)
    kf = k.astype(jnp.float32) * (dk ** -0.5)
    vf = v.astype(jnp.float32)
    qc = qf.reshape(B, n, C, H, dk).transpose(1, 0, 3, 2, 4)
    kc = kf.reshape(B, n, C, H, dk).transpose(1, 0, 3, 2, 4)
    vc = vf.reshape(B, n, C, H, dv).transpose(1, 0, 3, 2, 4)

    def step(state, inp):
        qi, ki, vi = inp
        scores = jnp.einsum('bhid,bhjd->bhij', qi, ki) * inner_decay
        inner = jnp.einsum('bhij,bhjv->bhiv', scores, vi)
        cross = jnp.einsum('bhid,bhdv->bhiv', qi * q_decay[None, :, :, None], state)
        state = state * chunk_decay[None, :, None, None] + jnp.einsum('bhjd,bhjv->bhdv', ki * k_decay[None, :, :, None], vi)
        return state, inner + cross

    state0 = jnp.zeros((B, H, dk, dv), jnp.float32)
    _, out = lax.scan(step, state0, (qc, kc, vc))
    return out.transpose(1, 0, 3, 2, 4).reshape(B, S, H, dv)


def t5_bucket(dist):
    n = jnp.maximum(dist, 0)
    max_exact = REL_BUCKETS // 2
    large = max_exact + (jnp.log(jnp.maximum(n, 1).astype(jnp.float32) / max_exact)
                         / math.log(REL_MAX_DIST / max_exact) * (REL_BUCKETS - max_exact)).astype(jnp.int32)
    large = jnp.minimum(large, REL_BUCKETS - 1)
    return jnp.where(n < max_exact, n, large)


def sparse_attention(q, k, v, iq, ik, iw, pos, rel_bias):
    B, S, H, dh = q.shape
    topk = min(TOPK_MAX, S // 4)
    nblk = S // Q_BLOCK
    iw = iw * (IDX_HEADS ** -0.5)

    def gather_rows(table, ids):
        return table[ids]

    def block(bi):
        start = bi * Q_BLOCK
        qb = lax.dynamic_slice_in_dim(q, start, Q_BLOCK, axis=1)
        iqb = lax.dynamic_slice_in_dim(iq, start, Q_BLOCK, axis=1)
        iwb = lax.dynamic_slice_in_dim(iw, start, Q_BLOCK, axis=1)
        pb = lax.dynamic_slice_in_dim(pos, start, Q_BLOCK, axis=1)
        ilog = jnp.einsum('bthd,bsd->bths', iqb, ik) * (IDX_DH ** -0.5)
        score = jnp.einsum('bth,bths->bts', iwb, jax.nn.relu(ilog)).astype(jnp.float32)
        causal = pos[:, None, :] <= pb[:, :, None]
        score = jnp.where(causal, score, -jnp.inf)
        _, sel = lax.top_k(score, topk)
        k_sel = jax.vmap(gather_rows)(k, sel)
        v_sel = jax.vmap(gather_rows)(v, sel)
        p_sel = jax.vmap(gather_rows)(pos, sel)
        dist = pb[:, :, None] - p_sel
        valid = dist >= 0
        bias = rel_bias[t5_bucket(dist)].astype(jnp.float32)
        logits = jnp.einsum('bthd,btkd->bhtk', qb, k_sel).astype(jnp.float32) * (dh ** -0.5)
        logits = logits + bias.transpose(0, 3, 1, 2)
        logits = jnp.where(valid[:, None], logits, -jnp.inf)
        p = jax.nn.softmax(logits, axis=-1).astype(v.dtype)
        return jnp.einsum('bhtk,btkd->bthd', p, v_sel)

    outs = lax.map(block, jnp.arange(nblk))
    return outs.transpose(1, 0, 2, 3, 4).reshape(B, S, H * dh)


def memory_attention(mq, mem, mem_g, w_kv, q_g, k_g):
    B, M, _ = mem.shape
    S = mq.shape[1]
    kv = rms_norm(mem, mem_g) @ w_kv
    mk, mv = jnp.split(kv, 2, axis=-1)
    mk = rms_norm(mk.reshape(B, M, MEM_HEADS, MEM_DH), k_g)
    mv = mv.reshape(B, M, MEM_HEADS, MEM_DH)
    mq = rms_norm(mq, q_g)
    logits = jnp.einsum('bshd,bmhd->bhsm', mq, mk).astype(jnp.float32) * (MEM_DH ** -0.5)
    p = jax.nn.softmax(logits, axis=-1).astype(mv.dtype)
    return jnp.einsum('bhsm,bmhd->bshd', p, mv).reshape(B, S, MEM_Q_W)


def peer(x, w_q, sub_keys, u_tab, v_tab):
    B, S, D = x.shape
    T = B * S
    K = PEER_TOPK
    xt = x.reshape(T, D)
    q = (xt @ w_q).reshape(T, PEER_HEADS, 2, PEER_DQ // 2)
    s = jnp.einsum('thcd,hcnd->thcn', q, sub_keys).astype(jnp.float32)
    sv, si = lax.top_k(s, K)
    cand = (sv[:, :, 0, :, None] + sv[:, :, 1, None, :]).reshape(T, PEER_HEADS, K * K)
    cand_idx = (si[:, :, 0, :, None] * PEER_NKEYS + si[:, :, 1, None, :]).reshape(T, PEER_HEADS, K * K)
    top_v, top_pos = lax.top_k(cand, K)
    eidx = jnp.take_along_axis(cand_idx, top_pos, axis=-1).reshape(T, PEER_HEADS * K)
    gate = jax.nn.softmax(top_v, axis=-1).reshape(T, PEER_HEADS * K).astype(x.dtype)
    nblk = T // PEER_BLOCK

    def block(inp):
        xb, ib, gb = inp
        act = jax.nn.gelu(jnp.einsum('td,ted->te', xb, u_tab[ib]), approximate=False)
        return jnp.einsum('te,ted->td', act * gb, v_tab[ib])

    out = lax.map(block, (xt.reshape(nblk, PEER_BLOCK, D),
                          eidx.reshape(nblk, PEER_BLOCK, PEER_HEADS * K),
                          gate.reshape(nblk, PEER_BLOCK, PEER_HEADS * K)))
    return out.reshape(B, S, D)


def setup_inputs(seed: int = 0) -> dict:
    key = jax.random.key(seed)
    ks = jax.random.split(key, 24)
    D = D_MODEL
    L = DEPTH
    f32 = jnp.float32

    def nrm(k, shape, scale):
        return jax.random.normal(k, shape, f32) * scale

    def gain(k, shape):
        return 1.0 + 0.02 * jax.random.normal(k, shape, f32)

    start = jax.random.randint(ks[2], (BATCH,), 0, 1024, dtype=jnp.int32)
    positions = (start[:, None] + jnp.arange(SEQ, dtype=jnp.int32)[None, :]).astype(jnp.int32)
    return {
        "x": nrm(ks[0], (BATCH, SEQ, D), 1.0),
        "mem": nrm(ks[1], (BATCH, MEM_LEN, D), 1.0),
        "positions": positions,
        "ln1_g": gain(ks[3], (L, D)),
        "w_in": nrm(ks[4], (L, D, IN_WIDTH), D ** -0.5),
        "att_q_g": gain(ks[5], (L, ATT_DH)),
        "att_k_g": gain(ks[6], (L, ATT_DH)),
        "rel_bias": nrm(ks[7], (REL_BUCKETS, ATT_HEADS), 0.5),
        "mem_g": gain(ks[8], (L, D)),
        "w_mem_kv": nrm(ks[9], (L, D, 2 * MEM_Q_W), D ** -0.5),
        "mem_q_g": gain(ks[10], (L, MEM_DH)),
        "mem_k_g": gain(ks[11], (L, MEM_DH)),
        "w_up_ret": nrm(ks[12], (L, RET_V_W, D), RET_V_W ** -0.5),
        "w_up_att": nrm(ks[13], (L, ATT_Q_W, D), ATT_Q_W ** -0.5),
        "w_up_mem": nrm(ks[14], (L, MEM_Q_W, D), MEM_Q_W ** -0.5),
        "w_out": nrm(ks[15], (L, D, D), D ** -0.5),
        "ln2_g": gain(ks[16], (L, D)),
        "peer_w_q": nrm(ks[17], (L, D, PEER_HEADS * PEER_DQ), D ** -0.5),
        "peer_sub_keys": nrm(ks[18], (L, PEER_HEADS, 2, PEER_NKEYS, PEER_DQ // 2), (PEER_DQ // 2) ** -0.5),
        "peer_u": nrm(ks[19], (L, PEER_NEXPERTS, D), D ** -0.5),
        "peer_v": nrm(ks[20], (L, PEER_NEXPERTS, D), 0.5),
    }


def reference(x, mem, positions, ln1_g, w_in, att_q_g, att_k_g, rel_bias, mem_g, w_mem_kv, mem_q_g, mem_k_g,
              w_up_ret, w_up_att, w_up_mem, w_out, ln2_g, peer_w_q, peer_sub_keys, peer_u, peer_v):
    B, S, D = x.shape
    for l in range(DEPTH):
        h = rms_norm(x, ln1_g[l])
        proj = h @ w_in[l]
        rq, rk, rv, rg, aq, ak, av, iq, ik, iw, mq, gates = jnp.split(proj, SPLIT_POINTS, axis=-1)
        rq = rotary(rq.reshape(B, S, RET_HEADS, RET_DK), positions)
        rk = rotary(rk.reshape(B, S, RET_HEADS, RET_DK), positions)
        ret = retention(rq, rk, rv.reshape(B, S, RET_HEADS, RET_DV))
        ret = _rms(ret).reshape(B, S, RET_V_W).astype(x.dtype)
        y_ret = jax.nn.silu(rg) * ret
        aq = rms_norm(aq.reshape(B, S, ATT_HEADS, ATT_DH), att_q_g[l])
        ak = rms_norm(ak, att_k_g[l])
        y_att = sparse_attention(aq, ak, av, iq.reshape(B, S, IDX_HEADS, IDX_DH), ik, iw, positions, rel_bias)
        y_mem = memory_attention(mq.reshape(B, S, MEM_HEADS, MEM_DH), mem, mem_g[l], w_mem_kv[l], mem_q_g[l], mem_k_g[l])
        g = jax.nn.sigmoid(gates.reshape(B, S, N_BRANCHES, D))
        merged = (g[:, :, 0] * (y_ret @ w_up_ret[l])
                  + g[:, :, 1] * (y_att @ w_up_att[l])
                  + g[:, :, 2] * (y_mem @ w_up_mem[l]))
        x = x + merged @ w_out[l]
        x = x + peer(rms_norm(x, ln2_g[l]), peer_w_q[l], peer_sub_keys[l], peer_u[l], peer_v[l])
    return x
```

```python
import functools
import math

import jax
import jax.numpy as jnp
import numpy as np
from jax import lax
from jax.experimental import pallas as pl
from jax.experimental.pallas import tpu as pltpu

F32 = jnp.float32
BF16 = jnp.bfloat16
I32 = jnp.int32

MEM_LEN = 256
RET_HEADS, RET_DK, RET_DV, RET_CHUNK = 8, 128, 256, 128
ATT_HEADS, ATT_DH = 8, 128
IDX_HEADS, IDX_DH = 16, 64
TOPK_MAX = 256
MEM_HEADS, MEM_DH = 4, 256
REL_BUCKETS, REL_MAX_DIST = 32, 128
PEER_HEADS, PEER_NKEYS, PEER_DQ, PEER_TOPK = 8, 128, 256, 16
EPS = 1e-6
ROPE_BASE = 10000.0

OFF_RQ = 0
OFF_RK = OFF_RQ + RET_HEADS * RET_DK
OFF_RV = OFF_RK + RET_HEADS * RET_DK
OFF_RG = OFF_RV + RET_HEADS * RET_DV
OFF_AQ = OFF_RG + RET_HEADS * RET_DV
OFF_AK = OFF_AQ + ATT_HEADS * ATT_DH
OFF_AV = OFF_AK + ATT_DH
OFF_IQ = OFF_AV + ATT_DH
OFF_IK = OFF_IQ + IDX_HEADS * IDX_DH
OFF_IW = OFF_IK + IDX_DH
OFF_MQ = OFF_IW + IDX_HEADS
OFF_GATE = OFF_MQ + MEM_HEADS * MEM_DH

LANE = 128
GATE_SHIFT = OFF_GATE % LANE
GATE_BASE = OFF_GATE - GATE_SHIFT
MQ_SHIFT = OFF_MQ - OFF_IK
VMEM_LIMIT = 56 * 1024 * 1024

INT_MIN = -(2 ** 31)
NEG_BIG = -1e30


def _cparams(sem):
    return pltpu.CompilerParams(dimension_semantics=sem, vmem_limit_bytes=VMEM_LIMIT)


def _dot_nt(a, b):
    return lax.dot_general(a, b, (((1,), (1,)), ((), ())), preferred_element_type=F32)


def _dot(a, b):
    return jnp.dot(a, b, preferred_element_type=F32)


def _rms(x):
    return x * lax.rsqrt(jnp.mean(x * x, axis=-1, keepdims=True) + EPS)


def _rmsnorm_kernel(x_ref, g_ref, o_ref):
    x = x_ref[...].astype(F32)
    o_ref[...] = (_rms(x) * g_ref[...]).astype(o_ref.dtype)


def _rmsnorm(x, g, tm):
    m, d = x.shape
    tm = min(tm, m)
    return pl.pallas_call(
        _rmsnorm_kernel,
        out_shape=jax.ShapeDtypeStruct((m, d), BF16),
        grid=(m // tm,),
        in_specs=[pl.BlockSpec((tm, d), lambda i: (i, 0)),
                  pl.BlockSpec((1, d), lambda i: (0, 0))],
        out_specs=pl.BlockSpec((tm, d), lambda i: (i, 0)),
        compiler_params=_cparams(("arbitrary",)),
        name="rmsnorm",
    )(x, g.reshape(1, d))


def _mm_kernel(*refs, has_res):
    if has_res:
        a_ref, w_ref, r_ref, o_ref, wsc = refs
    else:
        a_ref, w_ref, o_ref, wsc = refs

    @pl.when(pl.program_id(1) == 0)
    def _():
        wsc[...] = w_ref[...].astype(BF16)

    acc = _dot(a_ref[...], wsc[...])
    if has_res:
        acc = acc + r_ref[...]
    o_ref[...] = acc.astype(o_ref.dtype)


def _matmul(a, w, *, tm, tn, residual=None, out_dtype=F32, name="matmul"):
    m, k = a.shape
    n = w.shape[1]
    tm = min(tm, m)
    tn = min(tn, n)
    in_specs = [pl.BlockSpec((tm, k), lambda j, i: (i, 0)),
                pl.BlockSpec((k, tn), lambda j, i: (0, j))]
    args = [a, w]
    if residual is not None:
        in_specs.append(pl.BlockSpec((tm, tn), lambda j, i: (i, j)))
        args.append(residual)
    return pl.pallas_call(
        functools.partial(_mm_kernel, has_res=residual is not None),
        out_shape=jax.ShapeDtypeStruct((m, n), out_dtype),
        grid=(pl.cdiv(n, tn), m // tm),
        in_specs=in_specs,
        out_specs=pl.BlockSpec((tm, tn), lambda j, i: (i, j)),
        scratch_shapes=[pltpu.VMEM((k, tn), BF16)],
        compiler_params=_cparams(("arbitrary", "arbitrary")),
        name=name,
    )(*args)


def _ret_kernel(cdec_ref, rq_ref, rk_ref, rv_ref, rg_ref, pos_ref, invf_ref, sign_ref,
                idec_ref, qdec_ref, kdec_ref, o_ref, st_ref):
    @pl.when(pl.program_id(1) == 0)
    def _():
        st_ref[...] = jnp.zeros_like(st_ref)

    ang = pos_ref[...].astype(F32) * invf_ref[...]
    cos = jnp.cos(ang)
    sin = jnp.sin(ang) * sign_ref[...]
    scale = RET_DK ** -0.5
    for h in range(RET_HEADS):
        q = rq_ref[:, h * RET_DK:(h + 1) * RET_DK]
        k = rk_ref[:, h * RET_DK:(h + 1) * RET_DK]
        q = q * cos + pltpu.roll(q, RET_DK // 2, 1) * sin
        k = (k * cos + pltpu.roll(k, RET_DK // 2, 1) * sin) * scale
        vb = rv_ref[:, h * RET_DV:(h + 1) * RET_DV].astype(BF16)
        s = _dot_nt(q.astype(BF16), k.astype(BF16)) * idec_ref[h]
        inner = _dot(s.astype(BF16), vb)
        st = st_ref[h]
        cross = _dot((q * qdec_ref[h]).astype(BF16), st.astype(BF16))
        kd = (k * kdec_ref[h]).astype(BF16)
        upd = lax.dot_general(kd, vb, (((0,), (0,)), ((), ())), preferred_element_type=F32)
        st_ref[h] = st * cdec_ref[h] + upd
        y = _rms(inner + cross)
        g = rg_ref[:, h * RET_DV:(h + 1) * RET_DV]
        o_ref[:, h * RET_DV:(h + 1) * RET_DV] = (g / (1.0 + jnp.exp(-g)) * y).astype(o_ref.dtype)


def _retention(proj, positions, bsz, seq):
    c = RET_CHUNK
    nc = seq // c
    m = bsz * seq
    hh = RET_HEADS
    log_gamma = jnp.log1p(-jnp.exp2(-5.0 - jnp.arange(hh, dtype=F32)))
    idx = jnp.arange(c, dtype=F32)
    diff = idx[:, None] - idx[None, :]
    idec = jnp.where(diff[None] >= 0,
                     jnp.exp(log_gamma[:, None, None] * jnp.maximum(diff, 0.0)[None]), 0.0)
    qdec = jnp.exp(log_gamma[:, None] * (idx + 1.0)[None])
    kdec = jnp.exp(log_gamma[:, None] * (c - 1.0 - idx)[None])
    cdec = jnp.exp(log_gamma * c)
    qdec_b = jnp.broadcast_to(qdec[:, :, None], (hh, c, RET_DK))
    kdec_b = jnp.broadcast_to(kdec[:, :, None], (hh, c, RET_DK))
    half = RET_DK // 2
    inv_freq = 1.0 / (ROPE_BASE ** (jnp.arange(half, dtype=F32) / half))
    invf = jnp.concatenate([inv_freq, inv_freq]).reshape(1, RET_DK)
    sign = jnp.concatenate([-jnp.ones((half,), F32), jnp.ones((half,), F32)]).reshape(1, RET_DK)
    qk_w = hh * RET_DK
    v_w = hh * RET_DV
    row = lambda b, i: b * nc + i
    const3 = lambda b, i: (0, 0, 0)
    return pl.pallas_call(
        _ret_kernel,
        out_shape=jax.ShapeDtypeStruct((m, v_w), BF16),
        grid=(bsz, nc),
        in_specs=[
            pl.BlockSpec(memory_space=pltpu.SMEM),
            pl.BlockSpec((c, qk_w), lambda b, i: (row(b, i), OFF_RQ // qk_w)),
            pl.BlockSpec((c, qk_w), lambda b, i: (row(b, i), OFF_RK // qk_w)),
            pl.BlockSpec((c, v_w), lambda b, i: (row(b, i), OFF_RV // v_w)),
            pl.BlockSpec((c, v_w), lambda b, i: (row(b, i), OFF_RG // v_w)),
            pl.BlockSpec((c, 1), lambda b, i: (row(b, i), 0)),
            pl.BlockSpec((1, RET_DK), lambda b, i: (0, 0)),
            pl.BlockSpec((1, RET_DK), lambda b, i: (0, 0)),
            pl.BlockSpec((hh, c, c), const3),
            pl.BlockSpec((hh, c, RET_DK), const3),
            pl.BlockSpec((hh, c, RET_DK), const3),
        ],
        out_specs=pl.BlockSpec((c, v_w), lambda b, i: (row(b, i), 0)),
        scratch_shapes=[pltpu.VMEM((hh, RET_DK, RET_DV), F32)],
        compiler_params=_cparams(("arbitrary", "arbitrary")),
        name="retention",
    )(cdec, proj, proj, proj, proj, positions.reshape(m, 1), invf, sign, idec, qdec_b, kdec_b)


def _kprep_kernel(ak_ref, av_ref, ikw_ref, kg_ref, kn_ref, v_ref, ik2_ref):
    kn_ref[...] = (_rms(ak_ref[...]) * kg_ref[...]).astype(BF16)
    v_ref[...] = av_ref[...].astype(BF16)
    blk = ikw_ref[...]
    lane = lax.broadcasted_iota(I32, blk.shape, 1)
    ik2_ref[...] = jnp.where(lane < IDX_DH, blk, pltpu.roll(blk, IDX_DH, 1)).astype(BF16)


def _kprep(proj, att_k_g, tm):
    m = proj.shape[0]
    tm = min(tm, m)
    blk = lambda off: pl.BlockSpec((tm, LANE), lambda i: (i, off // LANE))
    out = jax.ShapeDtypeStruct((m, LANE), BF16)
    ospec = pl.BlockSpec((tm, LANE), lambda i: (i, 0))
    return pl.pallas_call(
        _kprep_kernel,
        out_shape=(out, out, out),
        grid=(m // tm,),
        in_specs=[blk(OFF_AK), blk(OFF_AV), blk(OFF_IK),
                  pl.BlockSpec((1, LANE), lambda i: (0, 0))],
        out_specs=(ospec, ospec, ospec),
        compiler_params=_cparams(("arbitrary",)),
        name="att_kprep",
    )(proj, proj, proj, att_k_g.reshape(1, ATT_DH))


IDX_TK = 512


def _att_kernel(aq_ref, iq0_ref, iq1_ref, iq2_ref, iq3_ref, ikw_ref, kn_ref, v_ref, ik2_ref,
                qg_ref, bias_ref, o_ref, sc_ref, *, topk, tq, tki):
    qi = pl.program_id(1)
    t0 = qi * tq
    nkb = (t0 + tq + tki - 1) // tki
    row_t = t0 + lax.broadcasted_iota(I32, (tq, 1), 0)

    lane = lax.broadcasted_iota(I32, (tq, LANE), 1)
    lo = lane < IDX_DH
    iq_refs = (iq0_ref, iq1_ref, iq2_ref, iq3_ref)
    qm = []
    wb = []
    for h in range(IDX_HEADS):
        pair = iq_refs[h // 4][:, ((h % 4) // 2) * LANE:((h % 4) // 2 + 1) * LANE]
        keep = lo if h % 2 == 0 else jnp.logical_not(lo)
        qm.append(jnp.where(keep, pair, 0.0).astype(BF16))
        wcol = ikw_ref[:, IDX_DH + h:IDX_DH + h + 1]
        wb.append(jnp.broadcast_to(wcol, (tq, LANE)))
    idx_scale = (IDX_DH ** -0.5) * (IDX_HEADS ** -0.5)

    def score_blk(kb, carry):
        k0 = pl.multiple_of(kb * tki, tki)
        kblk = ik2_ref[pl.ds(k0, tki), :]
        acc = jnp.zeros((tq, tki), F32)
        for h in range(IDX_HEADS):
            l = jnp.maximum(_dot_nt(qm[h], kblk), 0.0)
            acc = acc + jnp.concatenate([wb[h]] * (tki // LANE), axis=1) * l
        acc = acc * idx_scale
        s_idx = k0 + lax.broadcasted_iota(I32, (1, tki), 1)
        acc = jnp.where(s_idx <= row_t, acc, -jnp.inf)
        bits = pltpu.bitcast(acc, I32)
        sc_ref[:, pl.ds(k0, tki)] = jnp.where(bits < 0, bits ^ 0x7FFFFFFF, bits)
        return carry

    lax.fori_loop(0, nkb, score_blk, 0)

    def count_ge(cand):
        def body(kb, cnt):
            k0 = pl.multiple_of(kb * tki, tki)
            blk = sc_ref[:, pl.ds(k0, tki)]
            for c in range(tki // LANE):
                cnt = cnt + jnp.where(blk[:, c * LANE:(c + 1) * LANE] >= cand, 1, 0)
            return cnt
        cnt = lax.fori_loop(0, nkb, body, jnp.zeros((tq, LANE), I32))
        return jnp.sum(cnt, axis=1, keepdims=True)

    base = jnp.where(count_ge(jnp.zeros((tq, 1), I32)) >= topk, 0, INT_MIN).astype(I32)

    def bit_body(i, base):
        cand = base | jnp.left_shift(jnp.int32(1), 30 - i)
        return jnp.where(count_ge(cand) >= topk, cand, base)

    tau = lax.fori_loop(0, 31, bit_body, base)

    qg = qg_ref[...]
    qs = []
    for h in range(ATT_HEADS):
        qh = aq_ref[:, h * ATT_DH:(h + 1) * ATT_DH]
        qs.append((_rms(qh) * qg).astype(BF16))
    qall = jnp.concatenate(qs, axis=0)
    scale = ATT_DH ** -0.5
    nh = ATT_HEADS

    def att_blk(j, carry):
        m_i, l_i, acc = carry
        k0 = pl.multiple_of(j * LANE, LANE)
        kt = kn_ref[pl.ds(k0, LANE), :]
        vt = v_ref[pl.ds(k0, LANE), :]
        lg = _dot_nt(qall, kt) * scale
        bias = bias_ref[jnp.minimum(qi - j, 2)]
        key = sc_ref[:, pl.ds(k0, LANE)]
        s_idx = k0 + lax.broadcasted_iota(I32, (1, LANE), 1)
        sel = jnp.logical_and(key >= tau, s_idx <= row_t)
        lg = lg.reshape(nh, tq, LANE) + bias
        lg = jnp.where(sel[None], lg, NEG_BIG)
        m_new = jnp.maximum(m_i, jnp.max(lg, axis=-1, keepdims=True))
        alpha = jnp.exp(m_i - m_new)
        p = jnp.where(sel[None], jnp.exp(lg - m_new), 0.0)
        l_new = alpha * l_i + jnp.sum(p, axis=-1, keepdims=True)
        pv = _dot(p.reshape(nh * tq, LANE).astype(BF16), vt).reshape(nh, tq, ATT_DH)
        return m_new, l_new, alpha * acc + pv

    init = (jnp.full((nh, tq, 1), NEG_BIG, F32), jnp.zeros((nh, tq, 1), F32),
            jnp.zeros((nh, tq, ATT_DH), F32))
    _, l_f, acc_f = lax.fori_loop(0, qi + 1, att_blk, init)
    out = acc_f / l_f
    for h in range(ATT_HEADS):
        o_ref[:, h * ATT_DH:(h + 1) * ATT_DH] = out[h].astype(o_ref.dtype)


def _t5_bucket(dist):
    n = jnp.maximum(dist, 0)
    max_exact = REL_BUCKETS // 2
    large = max_exact + (jnp.log(jnp.maximum(n, 1).astype(F32) / max_exact)
                         / math.log(REL_MAX_DIST / max_exact) * (REL_BUCKETS - max_exact)).astype(I32)
    large = jnp.minimum(large, REL_BUCKETS - 1)
    return jnp.where(n < max_exact, n, large)


def _sparse_attention(proj, kn, vb, ik2, att_q_g, rel_bias, bsz, seq):
    tq = LANE
    m = bsz * seq
    nq = seq // tq
    topk = min(TOPK_MAX, seq // 4)
    tki = min(IDX_TK, seq)
    r = jnp.arange(tq, dtype=I32)[:, None]
    c = jnp.arange(LANE, dtype=I32)[None, :]
    dist = jnp.stack([d * LANE + r - c for d in range(3)])
    bias = rel_bias[_t5_bucket(dist)].astype(F32).transpose(0, 3, 1, 2)
    qw = ATT_HEADS * ATT_DH
    row = lambda b, i: b * nq + i
    iq_spec = lambda p: pl.BlockSpec((tq, 2 * LANE), lambda b, i: (row(b, i), OFF_IQ // (2 * LANE) + p))
    kspec = pl.BlockSpec((seq, LANE), lambda b, i: (b, 0))
    return pl.pallas_call(
        functools.partial(_att_kernel, topk=topk, tq=tq, tki=tki),
        out_shape=jax.ShapeDtypeStruct((m, qw), BF16),
        grid=(bsz, nq),
        in_specs=[
            pl.BlockSpec((tq, qw), lambda b, i: (row(b, i), OFF_AQ // qw)),
            iq_spec(0), iq_spec(1), iq_spec(2), iq_spec(3),
            pl.BlockSpec((tq, LANE), lambda b, i: (row(b, i), OFF_IK // LANE)),
            kspec, kspec, kspec,
            pl.BlockSpec((1, ATT_DH), lambda b, i: (0, 0)),
            pl.BlockSpec((3, ATT_HEADS, tq, LANE), lambda b, i: (0, 0, 0, 0)),
        ],
        out_specs=pl.BlockSpec((tq, qw), lambda b, i: (row(b, i), 0)),
        scratch_shapes=[pltpu.VMEM((tq, seq), I32)],
        compiler_params=_cparams(("arbitrary", "arbitrary")),
        name="sparse_attention",
    )(proj, proj, proj, proj, proj, proj, kn, vb, ik2, att_q_g.reshape(1, ATT_DH), bias)


MQ_WIN = 3 * LANE


def _mem_kernel(w0_ref, w1_ref, w2_ref, kv_ref, qg_ref, kg_ref, o_ref):
    win = jnp.concatenate([w0_ref[...], w1_ref[...], w2_ref[...]], axis=1)
    mq = win[:, MQ_SHIFT:MQ_SHIFT + MEM_HEADS * MEM_DH]
    qg = qg_ref[...]
    kg = kg_ref[...]
    scale = MEM_DH ** -0.5
    kvw = MEM_HEADS * MEM_DH
    for h in range(MEM_HEADS):
        q = (_rms(mq[:, h * MEM_DH:(h + 1) * MEM_DH]) * qg).astype(BF16)
        k = (_rms(kv_ref[:, h * MEM_DH:(h + 1) * MEM_DH]) * kg).astype(BF16)
        v = kv_ref[:, kvw + h * MEM_DH:kvw + (h + 1) * MEM_DH].astype(BF16)
        lg = _dot_nt(q, k) * scale
        p = jnp.exp(lg - jnp.max(lg, axis=-1, keepdims=True))
        p = p / jnp.sum(p, axis=-1, keepdims=True)
        o_ref[:, h * MEM_DH:(h + 1) * MEM_DH] = _dot(p.astype(BF16), v).astype(o_ref.dtype)


def _memory_attention(proj, kv, mem_q_g, mem_k_g, bsz, seq, ts):
    ts = min(ts, seq)
    ns = seq // ts
    m = bsz * seq
    qw = MEM_HEADS * MEM_DH
    row = lambda b, i: b * ns + i
    wspec = lambda p: pl.BlockSpec((ts, MQ_WIN), lambda b, i: (row(b, i), OFF_IK // MQ_WIN + p))
    return pl.pallas_call(
        _mem_kernel,
        out_shape=jax.ShapeDtypeStruct((m, qw), BF16),
        grid=(bsz, ns),
        in_specs=[wspec(0), wspec(1), wspec(2),
                  pl.BlockSpec((MEM_LEN, 2 * qw), lambda b, i: (b, 0)),
                  pl.BlockSpec((1, MEM_DH), lambda b, i: (0, 0)),
                  pl.BlockSpec((1, MEM_DH), lambda b, i: (0, 0))],
        out_specs=pl.BlockSpec((ts, qw), lambda b, i: (row(b, i), 0)),
        compiler_params=_cparams(("arbitrary", "arbitrary")),
        name="memory_attention",
    )(proj, proj, proj, kv, mem_q_g.reshape(1, MEM_DH), mem_k_g.reshape(1, MEM_DH))


MERGE_TN = 256


def _merge_kernel(yr_ref, ya_ref, ym_ref, wr_ref, wa_ref, wm_ref,
                  g0_ref, e0_ref, g1_ref, e1_ref, g2_ref, e2_ref, o_ref):
    acc = None
    for y_ref, w_ref, g_ref, e_ref in ((yr_ref, wr_ref, g0_ref, e0_ref),
                                       (ya_ref, wa_ref, g1_ref, e1_ref),
                                       (ym_ref, wm_ref, g2_ref, e2_ref)):
        g = jnp.concatenate([g_ref[...], e_ref[...]], axis=1)[:, GATE_SHIFT:GATE_SHIFT + MERGE_TN]
        y = _dot(y_ref[...], w_ref[...].astype(BF16))
        term = y / (1.0 + jnp.exp(-g))
        acc = term if acc is None else acc + term
    o_ref[...] = acc.astype(o_ref.dtype)


def _merge(proj, y_ret, y_att, y_mem, w_up_ret, w_up_att, w_up_mem, d_model, tm):
    m = y_ret.shape[0]
    tm = min(tm, m)
    tn = MERGE_TN
    yspec = lambda y: pl.BlockSpec((tm, y.shape[1]), lambda i, n: (i, 0))
    wspec = lambda w: pl.BlockSpec((w.shape[0], tn), lambda i, n: (0, n))
    gspecs = []
    for br in range(3):
        base = GATE_BASE + br * d_model
        gspecs.append(pl.BlockSpec((tm, tn), lambda i, n, base=base: (i, base // tn + n)))
        gspecs.append(pl.BlockSpec((tm, LANE), lambda i, n, base=base: (i, base // LANE + (n + 1) * (tn // LANE))))
    return pl.pallas_call(
        _merge_kernel,
        out_shape=jax.ShapeDtypeStruct((m, d_model), BF16),
        grid=(m // tm, d_model // tn),
        in_specs=[yspec(y_ret), yspec(y_att), yspec(y_mem),
                  wspec(w_up_ret), wspec(w_up_att), wspec(w_up_mem)] + gspecs,
        out_specs=pl.BlockSpec((tm, tn), lambda i, n: (i, n)),
        compiler_params=_cparams(("arbitrary", "arbitrary")),
        name="gated_merge",
    )(y_ret, y_att, y_mem, w_up_ret, w_up_att, w_up_mem, *([proj] * 6))


def _top_rows(vals, k):
    rows = []
    for _ in range(k):
        mx = jnp.max(vals, axis=0, keepdims=True)
        rows.append(mx)
        vals = jnp.where(vals == mx, -jnp.inf, vals)
    return rows


def _router_kernel(q_ref, keys_ref, s0_ref, s1_ref, st_ref):
    tops = []
    for c, s_ref in ((0, s0_ref), (1, s1_ref)):
        qs = q_ref[:, c * LANE:(c + 1) * LANE].astype(BF16)
        s_t = _dot_nt(keys_ref[c].astype(BF16), qs)
        s_ref[0] = s_t
        tops.append(_top_rows(s_t, PEER_TOPK))
    a, b = tops
    bmat = jnp.concatenate(b, axis=0)
    cand = jnp.concatenate([a[i] + bmat for i in range(PEER_TOPK)], axis=0)
    tv = _top_rows(cand, PEER_TOPK)
    z = None
    for r in range(PEER_TOPK):
        e = jnp.exp(tv[r] - tv[0])
        z = e if z is None else z + e
    zero = jnp.zeros_like(z)
    st_ref[0] = jnp.concatenate([tv[PEER_TOPK - 1], a[0], b[0], 1.0 / z, zero, zero, zero, zero], axis=0)


def _peer_router(q, sub_keys, tt):
    m = q.shape[0]
    tt = min(tt, m)
    nk = PEER_NKEYS
    keys = sub_keys.reshape(PEER_HEADS * 2, nk, PEER_DQ // 2)
    s_shape = jax.ShapeDtypeStruct((PEER_HEADS, nk, m), F32)
    s_spec = pl.BlockSpec((1, nk, tt), lambda i, h: (h, 0, i))
    return pl.pallas_call(
        _router_kernel,
        out_shape=(s_shape, s_shape, jax.ShapeDtypeStruct((PEER_HEADS, 8, m), F32)),
        grid=(m // tt, PEER_HEADS),
        in_specs=[pl.BlockSpec((tt, PEER_DQ), lambda i, h: (i, h)),
                  pl.BlockSpec((2, nk, PEER_DQ // 2), lambda i, h: (h, 0, 0))],
        out_specs=(s_spec, s_spec, pl.BlockSpec((1, 8, tt), lambda i, h: (h, 0, i))),
        compiler_params=_cparams(("arbitrary", "arbitrary")),
        name="peer_router",
    )(q, keys)


def _gelu(x):
    return 0.5 * x * (1.0 + lax.erf(x * (2.0 ** -0.5)))


def _peer_kernel(h_ref, u_ref, v_ref, s0_ref, s1_ref, st_ref, o_ref, e1_ref, *, te):
    e = pl.program_id(1)

    @pl.when(e == 0)
    def _():
        o_ref[...] = jnp.zeros_like(o_ref)
        for h in range(PEER_HEADS):
            e1_ref[h] = jnp.exp(s1_ref[h] - st_ref[h, 2:3, :])

    act_t = _gelu(_dot_nt(u_ref[...], h_ref[...]))
    nsub = te // PEER_NKEYS
    parts = []
    for ii in range(nsub):
        i_idx = e * nsub + ii
        w = None
        for h in range(PEER_HEADS):
            s0row = s0_ref[h, pl.ds(i_idx, 1), :]
            tau = st_ref[h, 0:1, :]
            e0row = jnp.exp(s0row - st_ref[h, 1:2, :]) * st_ref[h, 3:4, :]
            term = jnp.where(s0row + s1_ref[h] >= tau, e0row * e1_ref[h], 0.0)
            w = term if w is None else w + term
        parts.append(act_t[ii * PEER_NKEYS:(ii + 1) * PEER_NKEYS, :] * w)
    p_t = jnp.concatenate(parts, axis=0) if nsub > 1 else parts[0]
    p = p_t.T.astype(BF16)
    o_ref[...] += _dot(p, v_ref[...])


def _peer_experts(h2, u_bf, v_bf, s0t, s1t, stats, tm, te):
    m, d = h2.shape
    ne = u_bf.shape[0]
    tm = min(tm, m)
    sspec = pl.BlockSpec((PEER_HEADS, PEER_NKEYS, tm), lambda i, e: (0, 0, i))
    return pl.pallas_call(
        functools.partial(_peer_kernel, te=te),
        out_shape=jax.ShapeDtypeStruct((m, d), F32),
        grid=(m // tm, ne // te),
        in_specs=[pl.BlockSpec((tm, d), lambda i, e: (i, 0)),
                  pl.BlockSpec((te, d), lambda i, e: (e, 0)),
                  pl.BlockSpec((te, d), lambda i, e: (e, 0)),
                  sspec, sspec,
                  pl.BlockSpec((PEER_HEADS, 8, tm), lambda i, e: (0, 0, i))],
        out_specs=pl.BlockSpec((tm, d), lambda i, e: (i, 0)),
        scratch_shapes=[pltpu.VMEM((PEER_HEADS, PEER_NKEYS, tm), F32)],
        compiler_params=_cparams(("arbitrary", "arbitrary")),
        name="peer_experts",
    )(h2, u_bf, v_bf, s0t, s1t, stats)


def kernel(x, mem, positions, ln1_g, w_in, att_q_g, att_k_g, rel_bias, mem_g, w_mem_kv, mem_q_g, mem_k_g,
           w_up_ret, w_up_att, w_up_mem, w_out, ln2_g, peer_w_q, peer_sub_keys, peer_u, peer_v):
    bsz, seq, d = x.shape
    m = bsz * seq
    depth = ln1_g.shape[0]
    xf = x.reshape(m, d)
    memf = mem.reshape(bsz * mem.shape[1], d)
    for l in range(depth):
        h = _rmsnorm(xf, ln1_g[l], 256)
        proj = _matmul(h, w_in[l], tm=512, tn=768, name="in_proj")
        y_ret = _retention(proj, positions, bsz, seq)
        kn, vb, ik2 = _kprep(proj, att_k_g[l], 512)
        y_att = _sparse_attention(proj, kn, vb, ik2, att_q_g[l], rel_bias, bsz, seq)
        kv = _matmul(_rmsnorm(memf, mem_g[l], 256), w_mem_kv[l], tm=512, tn=512, name="mem_kv")
        y_mem = _memory_attention(proj, kv, mem_q_g[l], mem_k_g[l], bsz, seq, 512)
        merged = _merge(proj, y_ret, y_att, y_mem, w_up_ret[l], w_up_att[l], w_up_mem[l], d, 1024)
        xf = _matmul(merged, w_out[l], tm=512, tn=512, residual=xf, name="out_proj")
        h2 = _rmsnorm(xf, ln2_g[l], 256)
        q = _matmul(h2, peer_w_q[l], tm=512, tn=512, name="peer_query")
        s0t, s1t, stats = _peer_router(q, peer_sub_keys[l], 256)
        peer_out = _peer_experts(h2, peer_u[l].astype(BF16), peer_v[l].astype(BF16), s0t, s1t, stats, 512, 256)
        xf = xf + peer_out
    return xf.reshape(bsz, seq, d)
```

```python
import functools
import math

import jax
import jax.numpy as jnp
from jax import lax
from jax.experimental import pallas as pl
from jax.experimental.pallas import tpu as pltpu

F32 = jnp.float32
BF16 = jnp.bfloat16
I32 = jnp.int32

MEM_LEN = 256
RET_HEADS, RET_DK, RET_DV, RET_CHUNK = 8, 128, 256, 128
ATT_HEADS, ATT_DH = 8, 128
IDX_HEADS, IDX_DH = 16, 64
TOPK_MAX = 256
MEM_HEADS, MEM_DH = 4, 256
REL_BUCKETS, REL_MAX_DIST = 32, 128
PEER_HEADS, PEER_NKEYS, PEER_DQ, PEER_TOPK = 8, 128, 256, 16
EPS = 1e-6
ROPE_BASE = 10000.0

OFF_RQ = 0
OFF_RK = OFF_RQ + RET_HEADS * RET_DK
OFF_RV = OFF_RK + RET_HEADS * RET_DK
OFF_RG = OFF_RV + RET_HEADS * RET_DV
OFF_AQ = OFF_RG + RET_HEADS * RET_DV
OFF_AK = OFF_AQ + ATT_HEADS * ATT_DH
OFF_AV = OFF_AK + ATT_DH
OFF_IQ = OFF_AV + ATT_DH
OFF_IK = OFF_IQ + IDX_HEADS * IDX_DH
OFF_IW = OFF_IK + IDX_DH
OFF_MQ = OFF_IW + IDX_HEADS
OFF_GATE = OFF_MQ + MEM_HEADS * MEM_DH

LANE = 128
GATE_SHIFT = OFF_GATE % LANE
GATE_BASE = OFF_GATE - GATE_SHIFT
MQ_SHIFT = OFF_MQ - OFF_IK
VMEM_LIMIT = 56 * 1024 * 1024

INT_MIN = -(2 ** 31)
NEG_BIG = -1e30


def _cparams(sem):
    return pltpu.CompilerParams(dimension_semantics=sem, vmem_limit_bytes=VMEM_LIMIT)


def _dot_nt(a, b):
    return lax.dot_general(a, b, (((1,), (1,)), ((), ())), preferred_element_type=F32)


def _dot(a, b):
    return jnp.dot(a, b, preferred_element_type=F32)


def _rms(x):
    return x * lax.rsqrt(jnp.mean(x * x, axis=-1, keepdims=True) + EPS)


def _rmsnorm_kernel(x_ref, g_ref, o_ref, *ot_ref):
    y = _rms(x_ref[...].astype(F32)) * g_ref[...]
    o_ref[...] = y.astype(o_ref.dtype)
    if ot_ref:
        ot_ref[0][...] = y.T.astype(o_ref.dtype)


def _rmsnorm(x, g, tm, with_transposed=False):
    m, d = x.shape
    tm = min(tm, m)
    out_shape = [jax.ShapeDtypeStruct((m, d), BF16)]
    out_specs = [pl.BlockSpec((tm, d), lambda i: (i, 0))]
    if with_transposed:
        out_shape.append(jax.ShapeDtypeStruct((d, m), BF16))
        out_specs.append(pl.BlockSpec((d, tm), lambda i: (0, i)))
    res = pl.pallas_call(
        _rmsnorm_kernel,
        out_shape=tuple(out_shape),
        grid=(m // tm,),
        in_specs=[pl.BlockSpec((tm, d), lambda i: (i, 0)),
                  pl.BlockSpec((1, d), lambda i: (0, 0))],
        out_specs=tuple(out_specs),
        compiler_params=_cparams(("arbitrary",)),
        name="rmsnorm",
    )(x, g.reshape(1, d))
    return res if with_transposed else res[0]


def _mm_kernel(*refs, has_res, w_transposed):
    if has_res:
        a_ref, w_ref, r_ref, o_ref, wsc = refs
    else:
        a_ref, w_ref, o_ref, wsc = refs

    @pl.when(pl.program_id(1) == 0)
    def _():
        wsc[...] = w_ref[...].astype(BF16)

    acc = _dot_nt(a_ref[...], wsc[...]) if w_transposed else _dot(a_ref[...], wsc[...])
    if has_res:
        acc = acc + r_ref[...]
    o_ref[...] = acc.astype(o_ref.dtype)


def _matmul(a, w, *, tm, tn, residual=None, out_dtype=F32, w_transposed=False, name="matmul"):
    m, k = a.shape
    n = w.shape[0] if w_transposed else w.shape[1]
    tm = min(tm, m)
    tn = min(tn, n)
    if w_transposed:
        w_spec = pl.BlockSpec((tn, k), lambda j, i: (j, 0))
        w_scratch = pltpu.VMEM((tn, k), BF16)
    else:
        w_spec = pl.BlockSpec((k, tn), lambda j, i: (0, j))
        w_scratch = pltpu.VMEM((k, tn), BF16)
    in_specs = [pl.BlockSpec((tm, k), lambda j, i: (i, 0)), w_spec]
    args = [a, w]
    if residual is not None:
        in_specs.append(pl.BlockSpec((tm, tn), lambda j, i: (i, j)))
        args.append(residual)
    return pl.pallas_call(
        functools.partial(_mm_kernel, has_res=residual is not None, w_transposed=w_transposed),
        out_shape=jax.ShapeDtypeStruct((m, n), out_dtype),
        grid=(pl.cdiv(n, tn), m // tm),
        in_specs=in_specs,
        out_specs=pl.BlockSpec((tm, tn), lambda j, i: (i, j)),
        scratch_shapes=[w_scratch],
        compiler_params=_cparams(("arbitrary", "arbitrary")),
        name=name,
    )(*args)


def _ret_kernel(cdec_ref, rq_ref, rk_ref, rv_ref, rg_ref, pos_ref, invf_ref, sign_ref,
                idec_ref, qdec_ref, kdec_ref, o_ref, st_ref):
    @pl.when(pl.program_id(1) == 0)
    def _():
        st_ref[...] = jnp.zeros_like(st_ref)

    ang = pos_ref[...].astype(F32) * invf_ref[...]
    cos = jnp.cos(ang)
    sin = jnp.sin(ang) * sign_ref[...]
    scale = RET_DK ** -0.5
    for h in range(RET_HEADS):
        q = rq_ref[:, h * RET_DK:(h + 1) * RET_DK]
        k = rk_ref[:, h * RET_DK:(h + 1) * RET_DK]
        q = q * cos + pltpu.roll(q, RET_DK // 2, 1) * sin
        k = (k * cos + pltpu.roll(k, RET_DK // 2, 1) * sin) * scale
        vb = rv_ref[:, h * RET_DV:(h + 1) * RET_DV].astype(BF16)
        s = _dot_nt(q.astype(BF16), k.astype(BF16)) * idec_ref[h]
        inner = _dot(s.astype(BF16), vb)
        st = st_ref[h]
        cross = _dot((q * qdec_ref[h]).astype(BF16), st.astype(BF16))
        kd = (k * kdec_ref[h]).astype(BF16)
        upd = lax.dot_general(kd, vb, (((0,), (0,)), ((), ())), preferred_element_type=F32)
        st_ref[h] = st * cdec_ref[h] + upd
        y = _rms(inner + cross)
        g = rg_ref[:, h * RET_DV:(h + 1) * RET_DV]
        o_ref[:, h * RET_DV:(h + 1) * RET_DV] = (g / (1.0 + jnp.exp(-g)) * y).astype(o_ref.dtype)


def _retention(proj, positions, bsz, seq):
    c = RET_CHUNK
    nc = seq // c
    m = bsz * seq
    hh = RET_HEADS
    log_gamma = jnp.log1p(-jnp.exp2(-5.0 - jnp.arange(hh, dtype=F32)))
    idx = jnp.arange(c, dtype=F32)
    diff = idx[:, None] - idx[None, :]
    idec = jnp.where(diff[None] >= 0,
                     jnp.exp(log_gamma[:, None, None] * jnp.maximum(diff, 0.0)[None]), 0.0)
    qdec = jnp.exp(log_gamma[:, None] * (idx + 1.0)[None])
    kdec = jnp.exp(log_gamma[:, None] * (c - 1.0 - idx)[None])
    cdec = jnp.exp(log_gamma * c)
    qdec_b = jnp.broadcast_to(qdec[:, :, None], (hh, c, RET_DK))
    kdec_b = jnp.broadcast_to(kdec[:, :, None], (hh, c, RET_DK))
    half = RET_DK // 2
    inv_freq = 1.0 / (ROPE_BASE ** (jnp.arange(half, dtype=F32) / half))
    invf = jnp.concatenate([inv_freq, inv_freq]).reshape(1, RET_DK)
    sign = jnp.concatenate([-jnp.ones((half,), F32), jnp.ones((half,), F32)]).reshape(1, RET_DK)
    qk_w = hh * RET_DK
    v_w = hh * RET_DV
    row = lambda b, i: b * nc + i
    const3 = lambda b, i: (0, 0, 0)
    return pl.pallas_call(
        _ret_kernel,
        out_shape=jax.ShapeDtypeStruct((m, v_w), BF16),
        grid=(bsz, nc),
        in_specs=[
            pl.BlockSpec(memory_space=pltpu.SMEM),
            pl.BlockSpec((c, qk_w), lambda b, i: (row(b, i), OFF_RQ // qk_w)),
            pl.BlockSpec((c, qk_w), lambda b, i: (row(b, i), OFF_RK // qk_w)),
            pl.BlockSpec((c, v_w), lambda b, i: (row(b, i), OFF_RV // v_w)),
            pl.BlockSpec((c, v_w), lambda b, i: (row(b, i), OFF_RG // v_w)),
            pl.BlockSpec((c, 1), lambda b, i: (row(b, i), 0)),
            pl.BlockSpec((1, RET_DK), lambda b, i: (0, 0)),
            pl.BlockSpec((1, RET_DK), lambda b, i: (0, 0)),
            pl.BlockSpec((hh, c, c), const3),
            pl.BlockSpec((hh, c, RET_DK), const3),
            pl.BlockSpec((hh, c, RET_DK), const3),
        ],
        out_specs=pl.BlockSpec((c, v_w), lambda b, i: (row(b, i), 0)),
        scratch_shapes=[pltpu.VMEM((hh, RET_DK, RET_DV), F32)],
        compiler_params=_cparams(("arbitrary", "arbitrary")),
        name="retention",
    )(cdec, proj, proj, proj, proj, positions.reshape(m, 1), invf, sign, idec, qdec_b, kdec_b)


def _kprep_kernel(ak_ref, av_ref, ikw_ref, kg_ref, kn_ref, vt_ref, ik_ref):
    kn_ref[...] = (_rms(ak_ref[...]) * kg_ref[...]).astype(BF16)
    vt_ref[...] = av_ref[...].T.astype(BF16)
    ik_ref[...] = ikw_ref[...].astype(BF16)


def _kprep(proj, att_k_g, tm):
    m = proj.shape[0]
    tm = min(tm, m)
    blk = lambda off: pl.BlockSpec((tm, LANE), lambda i: (i, off // LANE))
    nat = jax.ShapeDtypeStruct((m, LANE), BF16)
    nat_spec = pl.BlockSpec((tm, LANE), lambda i: (i, 0))
    return pl.pallas_call(
        _kprep_kernel,
        out_shape=(nat, jax.ShapeDtypeStruct((LANE, m), BF16), nat),
        grid=(m // tm,),
        in_specs=[blk(OFF_AK), blk(OFF_AV), blk(OFF_IK),
                  pl.BlockSpec((1, LANE), lambda i: (0, 0))],
        out_specs=(nat_spec, pl.BlockSpec((LANE, tm), lambda i: (0, i)), nat_spec),
        compiler_params=_cparams(("arbitrary",)),
        name="att_kprep",
    )(proj, proj, proj, att_k_g.reshape(1, ATT_DH))


IDX_TK = 512
M_INIT = -1e29


def _att_kernel(aq_ref, iq0_ref, iq1_ref, iq2_ref, iq3_ref, ikw_ref, kn_ref, vt_ref, ik_ref,
                qg_ref, bias_ref, o_ref, sc_ref, qt_ref, acc_ref, *, topk, tq, tki):
    qi = pl.program_id(1)
    t0 = qi * tq
    nkb = (t0 + tq + tki - 1) // tki
    col_t = t0 + lax.broadcasted_iota(I32, (1, tq), 1)
    sub = tki // LANE

    heads_t = []
    for ref in (iq0_ref, iq1_ref, iq2_ref, iq3_ref):
        for c in range(2):
            pair_t = ref[:, c * LANE:(c + 1) * LANE].T
            heads_t.append(pair_t[:IDX_DH])
            heads_t.append(pair_t[IDX_DH:])
    rhs_all = jnp.concatenate(heads_t, axis=1).astype(BF16)
    w_t = ikw_ref[...].T
    idx_scale = (IDX_DH ** -0.5) * (IDX_HEADS ** -0.5)

    def score_blk(kb, carry):
        k0 = pl.multiple_of(kb * tki, tki)
        ikb = ik_ref[pl.ds(k0, tki), :][:, :IDX_DH]
        lg = _dot(ikb, rhs_all)
        acc = None
        for h in range(IDX_HEADS):
            term = w_t[IDX_DH + h:IDX_DH + h + 1, :] * jnp.maximum(lg[:, h * tq:(h + 1) * tq], 0.0)
            acc = term if acc is None else acc + term
        acc = acc * idx_scale
        s_idx = k0 + lax.broadcasted_iota(I32, (tki, 1), 0)
        acc = jnp.where(s_idx <= col_t, acc, -jnp.inf)
        bits = pltpu.bitcast(acc, I32)
        sc_ref[pl.ds(k0, tki), :] = jnp.where(bits < 0, bits ^ 0x7FFFFFFF, bits)
        return carry

    lax.fori_loop(0, nkb, score_blk, 0)

    def count_ge(cand):
        def body(kb, cnt):
            k0 = pl.multiple_of(kb * tki, tki)
            c = jnp.where(sc_ref[pl.ds(k0, tki), :] >= cand, 1, 0)
            return cnt + jnp.sum(c.reshape(tki // 8, 8, tq), axis=0)
        cnt = lax.fori_loop(0, nkb, body, jnp.zeros((8, tq), I32))
        return jnp.sum(cnt, axis=0, keepdims=True)

    base = jnp.where(count_ge(jnp.zeros((1, tq), I32)) >= topk, 0, INT_MIN).astype(I32)

    def bit_body(i, base):
        cand = base | jnp.left_shift(jnp.int32(1), 30 - i)
        return jnp.where(count_ge(cand) >= topk, cand, base)

    tau = lax.fori_loop(0, 31, bit_body, base)

    qg = qg_ref[...]
    scale = ATT_DH ** -0.5
    nfar = jnp.maximum(qi - 1, 0) // sub

    for h in range(ATT_HEADS):
        qt_ref[h] = (_rms(aq_ref[:, h * ATT_DH:(h + 1) * ATT_DH]) * qg).T.astype(BF16)
    acc_ref[...] = jnp.zeros_like(acc_ref)

    def softmax_block(carry, k0, width, sel, bias_of):
        m_all, l_all = carry
        kblk = kn_ref[pl.ds(k0, width), :]
        vblk = vt_ref[:, pl.ds(k0, width)]
        m_rows, l_rows = [], []
        for h in range(ATT_HEADS):
            lg = jnp.where(sel, _dot(kblk, qt_ref[h]) * scale + bias_of(h), NEG_BIG)
            m_i = m_all[h:h + 1]
            m_new = jnp.maximum(m_i, jnp.max(lg, axis=0, keepdims=True))
            alpha = jnp.exp(m_i - m_new)
            p = jnp.exp(lg - m_new)
            m_rows.append(m_new)
            l_rows.append(alpha * l_all[h:h + 1] + jnp.sum(p, axis=0, keepdims=True))
            acc_ref[h] = alpha * acc_ref[h] + _dot(vblk, p.astype(BF16))
        return jnp.concatenate(m_rows, axis=0), jnp.concatenate(l_rows, axis=0)

    def far_blk(kb, carry):
        k0 = pl.multiple_of(kb * tki, tki)
        sel = sc_ref[pl.ds(k0, tki), :] >= tau
        return softmax_block(carry, k0, tki, sel,
                             lambda h: jnp.concatenate([bias_ref[2, h]] * sub, axis=0))

    def near_blk(j, carry):
        k0 = pl.multiple_of(j * LANE, LANE)
        s_idx = k0 + lax.broadcasted_iota(I32, (LANE, 1), 0)
        sel = jnp.logical_and(sc_ref[pl.ds(k0, LANE), :] >= tau, s_idx <= col_t)
        d = jnp.minimum(qi - j, 2)
        return softmax_block(carry, k0, LANE, sel, lambda h: bias_ref[d, h])

    init = (jnp.full((ATT_HEADS, tq), M_INIT, F32), jnp.zeros((ATT_HEADS, tq), F32))
    carry = lax.fori_loop(0, nfar, far_blk, init)
    _, l_f = lax.fori_loop(nfar * sub, qi + 1, near_blk, carry)
    for h in range(ATT_HEADS):
        o_ref[:, h * ATT_DH:(h + 1) * ATT_DH] = (acc_ref[h] / l_f[h:h + 1]).T.astype(o_ref.dtype)


def _t5_bucket(dist):
    n = jnp.maximum(dist, 0)
    max_exact = REL_BUCKETS // 2
    large = max_exact + (jnp.log(jnp.maximum(n, 1).astype(F32) / max_exact)
                         / math.log(REL_MAX_DIST / max_exact) * (REL_BUCKETS - max_exact)).astype(I32)
    large = jnp.minimum(large, REL_BUCKETS - 1)
    return jnp.where(n < max_exact, n, large)


def _sparse_attention(proj, kn, vt, ikb, att_q_g, rel_bias, bsz, seq):
    tq = LANE
    m = bsz * seq
    nq = seq // tq
    topk = min(TOPK_MAX, seq // 4)
    tki = min(IDX_TK, seq)
    kk = jnp.arange(LANE, dtype=I32)[:, None]
    qq = jnp.arange(tq, dtype=I32)[None, :]
    dist = jnp.stack([d * LANE + qq - kk for d in range(3)])
    onehot = (_t5_bucket(dist)[..., None] == jnp.arange(REL_BUCKETS, dtype=I32)).astype(F32)
    bias = jnp.einsum("dkqb,bh->dhkq", onehot, rel_bias.astype(F32), precision=lax.Precision.HIGHEST)
    qw = ATT_HEADS * ATT_DH
    row = lambda b, i: b * nq + i
    iq_spec = lambda p: pl.BlockSpec((tq, 2 * LANE), lambda b, i: (row(b, i), OFF_IQ // (2 * LANE) + p))
    kspec = pl.BlockSpec((seq, LANE), lambda b, i: (b, 0))
    return pl.pallas_call(
        functools.partial(_att_kernel, topk=topk, tq=tq, tki=tki),
        out_shape=jax.ShapeDtypeStruct((m, qw), BF16),
        grid=(bsz, nq),
        in_specs=[
            pl.BlockSpec((tq, qw), lambda b, i: (row(b, i), OFF_AQ // qw)),
            iq_spec(0), iq_spec(1), iq_spec(2), iq_spec(3),
            pl.BlockSpec((tq, LANE), lambda b, i: (row(b, i), OFF_IK // LANE)),
            kspec,
            pl.BlockSpec((LANE, seq), lambda b, i: (0, b)),
            kspec,
            pl.BlockSpec((1, ATT_DH), lambda b, i: (0, 0)),
            pl.BlockSpec((3, ATT_HEADS, LANE, tq), lambda b, i: (0, 0, 0, 0)),
        ],
        out_specs=pl.BlockSpec((tq, qw), lambda b, i: (row(b, i), 0)),
        scratch_shapes=[pltpu.VMEM((seq, tq), I32),
                        pltpu.VMEM((ATT_HEADS, ATT_DH, tq), BF16),
                        pltpu.VMEM((ATT_HEADS, ATT_DH, tq), F32)],
        compiler_params=_cparams(("arbitrary", "arbitrary")),
        name="sparse_attention",
    )(proj, proj, proj, proj, proj, proj, kn, vt, ikb, att_q_g.reshape(1, ATT_DH), bias)


MQ_WIN = 3 * LANE


def _mem_kernel(w0_ref, w1_ref, w2_ref, kv_ref, qg_ref, kg_ref, o_ref):
    win = jnp.concatenate([w0_ref[...], w1_ref[...], w2_ref[...]], axis=1)
    mq = win[:, MQ_SHIFT:MQ_SHIFT + MEM_HEADS * MEM_DH]
    qg = qg_ref[...]
    kg = kg_ref[...]
    scale = MEM_DH ** -0.5
    kvw = MEM_HEADS * MEM_DH
    for h in range(MEM_HEADS):
        q = (_rms(mq[:, h * MEM_DH:(h + 1) * MEM_DH]) * qg).astype(BF16)
        k = (_rms(kv_ref[:, h * MEM_DH:(h + 1) * MEM_DH]) * kg).astype(BF16)
        v = kv_ref[:, kvw + h * MEM_DH:kvw + (h + 1) * MEM_DH].astype(BF16)
        lg = _dot_nt(q, k) * scale
        p = jnp.exp(lg - jnp.max(lg, axis=-1, keepdims=True))
        p = p / jnp.sum(p, axis=-1, keepdims=True)
        o_ref[:, h * MEM_DH:(h + 1) * MEM_DH] = _dot(p.astype(BF16), v).astype(o_ref.dtype)


def _memory_attention(proj, kv, mem_q_g, mem_k_g, bsz, seq, ts):
    ts = min(ts, seq)
    ns = seq // ts
    m = bsz * seq
    qw = MEM_HEADS * MEM_DH
    row = lambda b, i: b * ns + i
    wspec = lambda p: pl.BlockSpec((ts, MQ_WIN), lambda b, i: (row(b, i), OFF_IK // MQ_WIN + p))
    return pl.pallas_call(
        _mem_kernel,
        out_shape=jax.ShapeDtypeStruct((m, qw), BF16),
        grid=(bsz, ns),
        in_specs=[wspec(0), wspec(1), wspec(2),
                  pl.BlockSpec((MEM_LEN, 2 * qw), lambda b, i: (b, 0)),
                  pl.BlockSpec((1, MEM_DH), lambda b, i: (0, 0)),
                  pl.BlockSpec((1, MEM_DH), lambda b, i: (0, 0))],
        out_specs=pl.BlockSpec((ts, qw), lambda b, i: (row(b, i), 0)),
        compiler_params=_cparams(("arbitrary", "arbitrary")),
        name="memory_attention",
    )(proj, proj, proj, kv, mem_q_g.reshape(1, MEM_DH), mem_k_g.reshape(1, MEM_DH))


MERGE_TN = 256


def _merge_kernel(yr_ref, ya_ref, ym_ref, wr_ref, wa_ref, wm_ref,
                  g0_ref, e0_ref, g1_ref, e1_ref, g2_ref, e2_ref, o_ref):
    acc = None
    for y_ref, w_ref, g_ref, e_ref in ((yr_ref, wr_ref, g0_ref, e0_ref),
                                       (ya_ref, wa_ref, g1_ref, e1_ref),
                                       (ym_ref, wm_ref, g2_ref, e2_ref)):
        g = jnp.concatenate([g_ref[...], e_ref[...]], axis=1)[:, GATE_SHIFT:GATE_SHIFT + MERGE_TN]
        y = _dot(y_ref[...], w_ref[...].astype(BF16))
        term = y / (1.0 + jnp.exp(-g))
        acc = term if acc is None else acc + term
    o_ref[...] = acc.astype(o_ref.dtype)


def _merge(proj, y_ret, y_att, y_mem, w_up_ret, w_up_att, w_up_mem, d_model, tm):
    m = y_ret.shape[0]
    tm = min(tm, m)
    tn = MERGE_TN
    yspec = lambda y: pl.BlockSpec((tm, y.shape[1]), lambda i, n: (i, 0))
    wspec = lambda w: pl.BlockSpec((w.shape[0], tn), lambda i, n: (0, n))
    gspecs = []
    for br in range(3):
        base = GATE_BASE + br * d_model
        gspecs.append(pl.BlockSpec((tm, tn), lambda i, n, base=base: (i, base // tn + n)))
        gspecs.append(pl.BlockSpec((tm, LANE), lambda i, n, base=base: (i, base // LANE + (n + 1) * (tn // LANE))))
    return pl.pallas_call(
        _merge_kernel,
        out_shape=jax.ShapeDtypeStruct((m, d_model), BF16),
        grid=(m // tm, d_model // tn),
        in_specs=[yspec(y_ret), yspec(y_att), yspec(y_mem),
                  wspec(w_up_ret), wspec(w_up_att), wspec(w_up_mem)] + gspecs,
        out_specs=pl.BlockSpec((tm, tn), lambda i, n: (i, n)),
        compiler_params=_cparams(("arbitrary", "arbitrary")),
        name="gated_merge",
    )(y_ret, y_att, y_mem, w_up_ret, w_up_att, w_up_mem, *([proj] * 6))


def _top_rows(vals, k):
    rows = []
    for _ in range(k):
        mx = jnp.max(vals, axis=0, keepdims=True)
        rows.append(mx)
        vals = jnp.where(vals == mx, -jnp.inf, vals)
    return rows


def _router_kernel(q_ref, keys_ref, thr_ref, e0_ref, s1_ref, e1_ref):
    kk = PEER_TOPK
    s_t, tops = [], []
    for c in range(2):
        qs = q_ref[:, c * LANE:(c + 1) * LANE].astype(BF16)
        s_t.append(_dot_nt(keys_ref[c].astype(BF16), qs))
        tops.append(_top_rows(s_t[c], kk))
    a, b = tops
    amat = jnp.concatenate(a, axis=0)
    bmat = jnp.concatenate(b, axis=0)
    cand = jnp.concatenate([a[i] + bmat for i in range(kk)], axis=0)
    tv = _top_rows(cand, kk)
    tau = tv[kk - 1]
    z = None
    for r in range(kk):
        e = jnp.exp(tv[r] - tv[0])
        z = e if z is None else z + e
    cnt = jnp.zeros(amat.shape, I32)
    for j in range(kk):
        cnt = cnt + jnp.where(amat + b[j] >= tau, 1, 0)
    thr_rank = jnp.full(amat.shape, jnp.inf, F32)
    for j in range(kk):
        thr_rank = jnp.where(cnt == j + 1, b[j], thr_rank)
    thr = jnp.full(s_t[0].shape, jnp.inf, F32)
    for r in range(kk):
        thr = jnp.where(s_t[0] == a[r], thr_rank[r:r + 1], thr)
    thr_ref[0] = thr
    e0_ref[0] = jnp.exp(s_t[0] - a[0]) * (1.0 / z)
    s1_ref[0] = s_t[1]
    e1_ref[0] = jnp.exp(s_t[1] - b[0])


def _peer_router(q, sub_keys, tt):
    m = q.shape[0]
    tt = min(tt, m)
    nk = PEER_NKEYS
    keys = sub_keys.reshape(PEER_HEADS * 2, nk, PEER_DQ // 2)
    s_shape = jax.ShapeDtypeStruct((PEER_HEADS, nk, m), F32)
    s_spec = pl.BlockSpec((1, nk, tt), lambda i, h: (h, 0, i))
    return pl.pallas_call(
        _router_kernel,
        out_shape=(s_shape,) * 4,
        grid=(m // tt, PEER_HEADS),
        in_specs=[pl.BlockSpec((tt, PEER_DQ), lambda i, h: (i, h)),
                  pl.BlockSpec((2, nk, PEER_DQ // 2), lambda i, h: (h, 0, 0))],
        out_specs=(s_spec,) * 4,
        compiler_params=_cparams(("arbitrary", "arbitrary")),
        name="peer_router",
    )(q, keys)


def _gelu(x):
    return 0.5 * x * (1.0 + lax.erf(x * (2.0 ** -0.5)))


def _peer_gate_kernel(ht_ref, u_ref, thr_ref, e0_ref, s1_ref, e1_ref, o_ref, za_ref, zb_ref, *, te):
    e = pl.program_id(1)

    @pl.when(e == 0)
    def _():
        zb_ref[...] = jnp.zeros_like(zb_ref)

    nsub = te // PEER_NKEYS
    prev = jnp.maximum(e - 1, 0)
    tm = ht_ref.shape[1]

    def finish(z_ref):
        for ii in range(nsub):
            i_idx = prev * nsub + ii
            rows = slice(ii * PEER_NKEYS, (ii + 1) * PEER_NKEYS)
            thr_rows = [thr_ref[h, pl.ds(i_idx, 1), :] for h in range(PEER_HEADS)]
            e0_rows = [e0_ref[h, pl.ds(i_idx, 1), :] for h in range(PEER_HEADS)]
            for tc in range(tm // LANE):
                cols = slice(tc * LANE, (tc + 1) * LANE)
                w = None
                for h in range(PEER_HEADS):
                    hit = s1_ref[h, :, cols] >= thr_rows[h][:, cols]
                    term = jnp.where(hit, e0_rows[h][:, cols] * e1_ref[h, :, cols], 0.0)
                    w = term if w is None else w + term
                o_ref[cols, rows] = (_gelu(z_ref[rows, cols]) * w).T.astype(o_ref.dtype)

    def step(z_new, z_old):
        z_new[...] = _dot(u_ref[...], ht_ref[...])
        finish(z_old)

    @pl.when(e % 2 == 0)
    def _():
        step(za_ref, zb_ref)

    @pl.when(e % 2 == 1)
    def _():
        step(zb_ref, za_ref)


def _peer_gate(h2t, u_bf, thr0t, e0t, s1t, e1t, tm, te):
    d, m = h2t.shape
    ne = u_bf.shape[0]
    tm = min(tm, m)
    sspec = pl.BlockSpec((PEER_HEADS, PEER_NKEYS, tm), lambda i, e: (0, 0, i))
    nblk = ne // te
    return pl.pallas_call(
        functools.partial(_peer_gate_kernel, te=te),
        out_shape=jax.ShapeDtypeStruct((m, ne), BF16),
        grid=(m // tm, nblk + 1),
        in_specs=[pl.BlockSpec((d, tm), lambda i, e: (0, i)),
                  pl.BlockSpec((te, d), lambda i, e: (jnp.minimum(e, nblk - 1), 0)),
                  sspec, sspec, sspec, sspec],
        out_specs=pl.BlockSpec((tm, te), lambda i, e: (i, jnp.maximum(e - 1, 0))),
        scratch_shapes=[pltpu.VMEM((te, tm), F32), pltpu.VMEM((te, tm), F32)],
        compiler_params=_cparams(("arbitrary", "arbitrary")),
        name="peer_gate",
    )(h2t, u_bf, thr0t, e0t, s1t, e1t)


def _peer_out_kernel(p_ref, v_ref, x_ref, o_ref):
    @pl.when(pl.program_id(2) == 0)
    def _():
        o_ref[...] = x_ref[...]

    o_ref[...] += _dot(p_ref[...], v_ref[...])


def _peer_out(p, v_bf, x, tm, tn, tk):
    m, ne = p.shape
    d = v_bf.shape[1]
    tm, tn, tk = min(tm, m), min(tn, d), min(tk, ne)
    return pl.pallas_call(
        _peer_out_kernel,
        out_shape=jax.ShapeDtypeStruct((m, d), F32),
        grid=(m // tm, d // tn, ne // tk),
        in_specs=[pl.BlockSpec((tm, tk), lambda i, n, k: (i, k)),
                  pl.BlockSpec((tk, tn), lambda i, n, k: (k, n)),
                  pl.BlockSpec((tm, tn), lambda i, n, k: (i, n))],
        out_specs=pl.BlockSpec((tm, tn), lambda i, n, k: (i, n)),
        compiler_params=_cparams(("arbitrary", "arbitrary", "arbitrary")),
        name="peer_out",
    )(p, v_bf, x)


def kernel(x, mem, positions, ln1_g, w_in, att_q_g, att_k_g, rel_bias, mem_g, w_mem_kv, mem_q_g, mem_k_g,
           w_up_ret, w_up_att, w_up_mem, w_out, ln2_g, peer_w_q, peer_sub_keys, peer_u, peer_v):
    bsz, seq, d = x.shape
    m = bsz * seq
    depth = ln1_g.shape[0]
    xf = x.reshape(m, d)
    memf = mem.reshape(bsz * mem.shape[1], d)
    for l in range(depth):
        h = _rmsnorm(xf, ln1_g[l], 256)
        proj = _matmul(h, jnp.swapaxes(w_in[l], 0, 1), tm=512, tn=768, w_transposed=True, name="in_proj")
        y_ret = _retention(proj, positions, bsz, seq)
        kn, vt, ikb = _kprep(proj, att_k_g[l], 512)
        y_att = _sparse_attention(proj, kn, vt, ikb, att_q_g[l], rel_bias, bsz, seq)
        kv = _matmul(_rmsnorm(memf, mem_g[l], 256), w_mem_kv[l], tm=512, tn=512, name="mem_kv")
        y_mem = _memory_attention(proj, kv, mem_q_g[l], mem_k_g[l], bsz, seq, 512)
        merged = _merge(proj, y_ret, y_att, y_mem, w_up_ret[l], w_up_att[l], w_up_mem[l], d, 1024)
        xf = _matmul(merged, w_out[l], tm=512, tn=512, residual=xf, name="out_proj")
        h2, h2t = _rmsnorm(xf, ln2_g[l], 256, with_transposed=True)
        q = _matmul(h2, peer_w_q[l], tm=512, tn=512, name="peer_query")
        thr0t, e0t, s1t, e1t = _peer_router(q, peer_sub_keys[l], 256)
        p = _peer_gate(h2t, peer_u[l].astype(BF16), thr0t, e0t, s1t, e1t, 512, 512)
        xf = _peer_out(p, peer_v[l].astype(BF16), xf, 512, 2048, 2048)
    return xf.reshape(bsz, seq, d)
```

```python
import functools
import math

import jax
import jax.numpy as jnp
from jax import lax
from jax.experimental import pallas as pl
from jax.experimental.pallas import tpu as pltpu

F32 = jnp.float32
BF16 = jnp.bfloat16
I32 = jnp.int32

MEM_LEN = 256
RET_HEADS, RET_DK, RET_DV, RET_CHUNK = 8, 128, 256, 128
ATT_HEADS, ATT_DH = 8, 128
IDX_HEADS, IDX_DH = 16, 64
TOPK_MAX = 256
MEM_HEADS, MEM_DH = 4, 256
REL_BUCKETS, REL_MAX_DIST = 32, 128
PEER_HEADS, PEER_NKEYS, PEER_DQ, PEER_TOPK = 8, 128, 256, 16
EPS = 1e-6
ROPE_BASE = 10000.0

OFF_RQ = 0
OFF_RK = OFF_RQ + RET_HEADS * RET_DK
OFF_RV = OFF_RK + RET_HEADS * RET_DK
OFF_RG = OFF_RV + RET_HEADS * RET_DV
OFF_AQ = OFF_RG + RET_HEADS * RET_DV
OFF_AK = OFF_AQ + ATT_HEADS * ATT_DH
OFF_AV = OFF_AK + ATT_DH
OFF_IQ = OFF_AV + ATT_DH
OFF_IK = OFF_IQ + IDX_HEADS * IDX_DH
OFF_IW = OFF_IK + IDX_DH
OFF_MQ = OFF_IW + IDX_HEADS
OFF_GATE = OFF_MQ + MEM_HEADS * MEM_DH

LANE = 128
GATE_SHIFT = OFF_GATE % LANE
GATE_BASE = OFF_GATE - GATE_SHIFT
MQ_SHIFT = OFF_MQ - OFF_IK
VMEM_LIMIT = 56 * 1024 * 1024

INT_MIN = -(2 ** 31)
NEG_BIG = -1e30


def _cparams(sem):
    return pltpu.CompilerParams(dimension_semantics=sem, vmem_limit_bytes=VMEM_LIMIT)


def _dot_nt(a, b):
    return lax.dot_general(a, b, (((1,), (1,)), ((), ())), preferred_element_type=F32)


def _dot(a, b):
    return jnp.dot(a, b, preferred_element_type=F32)


def _rms(x):
    return x * lax.rsqrt(jnp.mean(x * x, axis=-1, keepdims=True) + EPS)


def _rmsnorm_kernel(x_ref, g_ref, o_ref, *ot_ref):
    y = _rms(x_ref[...].astype(F32)) * g_ref[...]
    o_ref[...] = y.astype(o_ref.dtype)
    if ot_ref:
        ot_ref[0][...] = y.T.astype(o_ref.dtype)


def _rmsnorm(x, g, tm, with_transposed=False):
    m, d = x.shape
    tm = min(tm, m)
    out_shape = [jax.ShapeDtypeStruct((m, d), BF16)]
    out_specs = [pl.BlockSpec((tm, d), lambda i: (i, 0))]
    if with_transposed:
        out_shape.append(jax.ShapeDtypeStruct((d, m), BF16))
        out_specs.append(pl.BlockSpec((d, tm), lambda i: (0, i)))
    res = pl.pallas_call(
        _rmsnorm_kernel,
        out_shape=tuple(out_shape),
        grid=(m // tm,),
        in_specs=[pl.BlockSpec((tm, d), lambda i: (i, 0)),
                  pl.BlockSpec((1, d), lambda i: (0, 0))],
        out_specs=tuple(out_specs),
        compiler_params=_cparams(("arbitrary",)),
        name="rmsnorm",
    )(x, g.reshape(1, d))
    return res if with_transposed else res[0]


def _mm_kernel(*refs, has_res, w_transposed):
    if has_res:
        a_ref, w_ref, r_ref, o_ref, wsc = refs
    else:
        a_ref, w_ref, o_ref, wsc = refs

    @pl.when(pl.program_id(1) == 0)
    def _():
        wsc[...] = w_ref[...].astype(BF16)

    acc = _dot_nt(a_ref[...], wsc[...]) if w_transposed else _dot(a_ref[...], wsc[...])
    if has_res:
        acc = acc + r_ref[...]
    o_ref[...] = acc.astype(o_ref.dtype)


def _matmul(a, w, *, tm, tn, residual=None, out_dtype=F32, w_transposed=False, name="matmul"):
    m, k = a.shape
    n = w.shape[0] if w_transposed else w.shape[1]
    tm = min(tm, m)
    tn = min(tn, n)
    if w_transposed:
        w_spec = pl.BlockSpec((tn, k), lambda j, i: (j, 0))
        w_scratch = pltpu.VMEM((tn, k), BF16)
    else:
        w_spec = pl.BlockSpec((k, tn), lambda j, i: (0, j))
        w_scratch = pltpu.VMEM((k, tn), BF16)
    in_specs = [pl.BlockSpec((tm, k), lambda j, i: (i, 0)), w_spec]
    args = [a, w]
    if residual is not None:
        in_specs.append(pl.BlockSpec((tm, tn), lambda j, i: (i, j)))
        args.append(residual)
    return pl.pallas_call(
        functools.partial(_mm_kernel, has_res=residual is not None, w_transposed=w_transposed),
        out_shape=jax.ShapeDtypeStruct((m, n), out_dtype),
        grid=(pl.cdiv(n, tn), m // tm),
        in_specs=in_specs,
        out_specs=pl.BlockSpec((tm, tn), lambda j, i: (i, j)),
        scratch_shapes=[w_scratch],
        compiler_params=_cparams(("arbitrary", "arbitrary")),
        name=name,
    )(*args)


def _ret_kernel(cdec_ref, rq_ref, rk_ref, rv_ref, rg_ref, pos_ref, invf_ref, sign_ref,
                idec_ref, qdec_ref, kdec_ref, o_ref, st_ref):
    @pl.when(pl.program_id(1) == 0)
    def _():
        st_ref[...] = jnp.zeros_like(st_ref)

    ang = pos_ref[...].astype(F32) * invf_ref[...]
    cos = jnp.cos(ang)
    sin = jnp.sin(ang) * sign_ref[...]
    scale = RET_DK ** -0.5
    for h in range(RET_HEADS):
        q = rq_ref[:, h * RET_DK:(h + 1) * RET_DK]
        k = rk_ref[:, h * RET_DK:(h + 1) * RET_DK]
        q = q * cos + pltpu.roll(q, RET_DK // 2, 1) * sin
        k = (k * cos + pltpu.roll(k, RET_DK // 2, 1) * sin) * scale
        vb = rv_ref[:, h * RET_DV:(h + 1) * RET_DV].astype(BF16)
        s = _dot_nt(q.astype(BF16), k.astype(BF16)) * idec_ref[h]
        inner = _dot(s.astype(BF16), vb)
        st = st_ref[h]
        cross = _dot((q * qdec_ref[h]).astype(BF16), st.astype(BF16))
        kd = (k * kdec_ref[h]).astype(BF16)
        upd = lax.dot_general(kd, vb, (((0,), (0,)), ((), ())), preferred_element_type=F32)
        st_ref[h] = st * cdec_ref[h] + upd
        y = _rms(inner + cross)
        g = rg_ref[:, h * RET_DV:(h + 1) * RET_DV]
        o_ref[:, h * RET_DV:(h + 1) * RET_DV] = (g / (1.0 + jnp.exp(-g)) * y).astype(o_ref.dtype)


def _retention(proj, positions, bsz, seq):
    c = RET_CHUNK
    nc = seq // c
    m = bsz * seq
    hh = RET_HEADS
    log_gamma = jnp.log1p(-jnp.exp2(-5.0 - jnp.arange(hh, dtype=F32)))
    idx = jnp.arange(c, dtype=F32)
    diff = idx[:, None] - idx[None, :]
    idec = jnp.where(diff[None] >= 0,
                     jnp.exp(log_gamma[:, None, None] * jnp.maximum(diff, 0.0)[None]), 0.0)
    qdec = jnp.exp(log_gamma[:, None] * (idx + 1.0)[None])
    kdec = jnp.exp(log_gamma[:, None] * (c - 1.0 - idx)[None])
    cdec = jnp.exp(log_gamma * c)
    qdec_b = jnp.broadcast_to(qdec[:, :, None], (hh, c, RET_DK))
    kdec_b = jnp.broadcast_to(kdec[:, :, None], (hh, c, RET_DK))
    half = RET_DK // 2
    inv_freq = 1.0 / (ROPE_BASE ** (jnp.arange(half, dtype=F32) / half))
    invf = jnp.concatenate([inv_freq, inv_freq]).reshape(1, RET_DK)
    sign = jnp.concatenate([-jnp.ones((half,), F32), jnp.ones((half,), F32)]).reshape(1, RET_DK)
    qk_w = hh * RET_DK
    v_w = hh * RET_DV
    row = lambda b, i: b * nc + i
    const3 = lambda b, i: (0, 0, 0)
    return pl.pallas_call(
        _ret_kernel,
        out_shape=jax.ShapeDtypeStruct((m, v_w), BF16),
        grid=(bsz, nc),
        in_specs=[
            pl.BlockSpec(memory_space=pltpu.SMEM),
            pl.BlockSpec((c, qk_w), lambda b, i: (row(b, i), OFF_RQ // qk_w)),
            pl.BlockSpec((c, qk_w), lambda b, i: (row(b, i), OFF_RK // qk_w)),
            pl.BlockSpec((c, v_w), lambda b, i: (row(b, i), OFF_RV // v_w)),
            pl.BlockSpec((c, v_w), lambda b, i: (row(b, i), OFF_RG // v_w)),
            pl.BlockSpec((c, 1), lambda b, i: (row(b, i), 0)),
            pl.BlockSpec((1, RET_DK), lambda b, i: (0, 0)),
            pl.BlockSpec((1, RET_DK), lambda b, i: (0, 0)),
            pl.BlockSpec((hh, c, c), const3),
            pl.BlockSpec((hh, c, RET_DK), const3),
            pl.BlockSpec((hh, c, RET_DK), const3),
        ],
        out_specs=pl.BlockSpec((c, v_w), lambda b, i: (row(b, i), 0)),
        scratch_shapes=[pltpu.VMEM((hh, RET_DK, RET_DV), F32)],
        compiler_params=_cparams(("arbitrary", "arbitrary")),
        name="retention",
    )(cdec, proj, proj, proj, proj, positions.reshape(m, 1), invf, sign, idec, qdec_b, kdec_b)


def _kprep_kernel(ak_ref, av_ref, ikw_ref, kg_ref, kn_ref, vt_ref, ik_ref):
    kn_ref[...] = (_rms(ak_ref[...]) * kg_ref[...]).astype(BF16)
    vt_ref[...] = av_ref[...].T.astype(BF16)
    ik_ref[...] = ikw_ref[...].astype(BF16)


def _kprep(proj, att_k_g, tm):
    m = proj.shape[0]
    tm = min(tm, m)
    blk = lambda off: pl.BlockSpec((tm, LANE), lambda i: (i, off // LANE))
    nat = jax.ShapeDtypeStruct((m, LANE), BF16)
    nat_spec = pl.BlockSpec((tm, LANE), lambda i: (i, 0))
    return pl.pallas_call(
        _kprep_kernel,
        out_shape=(nat, jax.ShapeDtypeStruct((LANE, m), BF16), nat),
        grid=(m // tm,),
        in_specs=[blk(OFF_AK), blk(OFF_AV), blk(OFF_IK),
                  pl.BlockSpec((1, LANE), lambda i: (0, 0))],
        out_specs=(nat_spec, pl.BlockSpec((LANE, tm), lambda i: (0, i)), nat_spec),
        compiler_params=_cparams(("arbitrary",)),
        name="att_kprep",
    )(proj, proj, proj, att_k_g.reshape(1, ATT_DH))


IDX_TK = 512
M_INIT = -1e29


def _att_kernel(aq_ref, iq0_ref, iq1_ref, iq2_ref, iq3_ref, ikw_ref, kn_ref, vt_ref, ik_ref,
                qg_ref, bias_ref, o_ref, sc_ref, qt_ref, acc_ref, *, topk, tq, tki):
    qi = pl.program_id(1)
    t0 = qi * tq
    nkb = (t0 + tq + tki - 1) // tki
    col_t = t0 + lax.broadcasted_iota(I32, (1, tq), 1)
    sub = tki // LANE

    heads_t = []
    for ref in (iq0_ref, iq1_ref, iq2_ref, iq3_ref):
        for c in range(2):
            pair_t = ref[:, c * LANE:(c + 1) * LANE].T
            heads_t.append(pair_t[:IDX_DH])
            heads_t.append(pair_t[IDX_DH:])
    rhs_all = jnp.concatenate(heads_t, axis=1).astype(BF16)
    w_t = ikw_ref[...].T
    idx_scale = (IDX_DH ** -0.5) * (IDX_HEADS ** -0.5)

    def score_blk(kb, carry):
        k0 = pl.multiple_of(kb * tki, tki)
        ikb = ik_ref[pl.ds(k0, tki), :][:, :IDX_DH]
        lg = _dot(ikb, rhs_all)
        acc = None
        for h in range(IDX_HEADS):
            term = w_t[IDX_DH + h:IDX_DH + h + 1, :] * jnp.maximum(lg[:, h * tq:(h + 1) * tq], 0.0)
            acc = term if acc is None else acc + term
        acc = acc * idx_scale
        s_idx = k0 + lax.broadcasted_iota(I32, (tki, 1), 0)
        acc = jnp.where(s_idx <= col_t, acc, -jnp.inf)
        bits = pltpu.bitcast(acc, I32)
        sc_ref[pl.ds(k0, tki), :] = jnp.where(bits < 0, bits ^ 0x7FFFFFFF, bits)
        return carry

    lax.fori_loop(0, nkb, score_blk, 0)

    def count_ge(cand):
        def body(kb, cnt):
            k0 = pl.multiple_of(kb * tki, tki)
            c = jnp.where(sc_ref[pl.ds(k0, tki), :] >= cand, 1, 0)
            return cnt + jnp.sum(c.reshape(tki // 8, 8, tq), axis=0)
        cnt = lax.fori_loop(0, nkb, body, jnp.zeros((8, tq), I32))
        return jnp.sum(cnt, axis=0, keepdims=True)

    base = jnp.where(count_ge(jnp.zeros((1, tq), I32)) >= topk, 0, INT_MIN).astype(I32)

    def bit_body(i, base):
        cand = base | jnp.left_shift(jnp.int32(1), 30 - i)
        return jnp.where(count_ge(cand) >= topk, cand, base)

    tau = lax.fori_loop(0, 31, bit_body, base)

    qg = qg_ref[...]
    scale = ATT_DH ** -0.5
    nfar = jnp.maximum(qi - 1, 0) // sub

    for h in range(ATT_HEADS):
        qt_ref[h] = (_rms(aq_ref[:, h * ATT_DH:(h + 1) * ATT_DH]) * qg).T.astype(BF16)
    acc_ref[...] = jnp.zeros_like(acc_ref)

    def softmax_block(carry, k0, width, sel, bias_of):
        m_all, l_all = carry
        kblk = kn_ref[pl.ds(k0, width), :]
        vblk = vt_ref[:, pl.ds(k0, width)]
        m_rows, l_rows = [], []
        for h in range(ATT_HEADS):
            lg = jnp.where(sel, _dot(kblk, qt_ref[h]) * scale + bias_of(h), NEG_BIG)
            m_i = m_all[h:h + 1]
            m_new = jnp.maximum(m_i, jnp.max(lg, axis=0, keepdims=True))
            alpha = jnp.exp(m_i - m_new)
            p = jnp.exp(lg - m_new)
            m_rows.append(m_new)
            l_rows.append(alpha * l_all[h:h + 1] + jnp.sum(p, axis=0, keepdims=True))
            acc_ref[h] = alpha * acc_ref[h] + _dot(vblk, p.astype(BF16))
        return jnp.concatenate(m_rows, axis=0), jnp.concatenate(l_rows, axis=0)

    def far_blk(kb, carry):
        k0 = pl.multiple_of(kb * tki, tki)
        sel = sc_ref[pl.ds(k0, tki), :] >= tau
        return softmax_block(carry, k0, tki, sel,
                             lambda h: jnp.concatenate([bias_ref[2, h]] * sub, axis=0))

    def near_blk(j, carry):
        k0 = pl.multiple_of(j * LANE, LANE)
        s_idx = k0 + lax.broadcasted_iota(I32, (LANE, 1), 0)
        sel = jnp.logical_and(sc_ref[pl.ds(k0, LANE), :] >= tau, s_idx <= col_t)
        d = jnp.minimum(qi - j, 2)
        return softmax_block(carry, k0, LANE, sel, lambda h: bias_ref[d, h])

    init = (jnp.full((ATT_HEADS, tq), M_INIT, F32), jnp.zeros((ATT_HEADS, tq), F32))
    carry = lax.fori_loop(0, nfar, far_blk, init)
    _, l_f = lax.fori_loop(nfar * sub, qi + 1, near_blk, carry)
    for h in range(ATT_HEADS):
        o_ref[:, h * ATT_DH:(h + 1) * ATT_DH] = (acc_ref[h] / l_f[h:h + 1]).T.astype(o_ref.dtype)


def _t5_bucket(dist):
    n = jnp.maximum(dist, 0)
    max_exact = REL_BUCKETS // 2
    large = max_exact + (jnp.log(jnp.maximum(n, 1).astype(F32) / max_exact)
                         / math.log(REL_MAX_DIST / max_exact) * (REL_BUCKETS - max_exact)).astype(I32)
    large = jnp.minimum(large, REL_BUCKETS - 1)
    return jnp.where(n < max_exact, n, large)


def _sparse_attention(proj, kn, vt, ikb, att_q_g, rel_bias, bsz, seq):
    tq = LANE
    m = bsz * seq
    nq = seq // tq
    topk = min(TOPK_MAX, seq // 4)
    tki = min(IDX_TK, seq)
    kk = jnp.arange(LANE, dtype=I32)[:, None]
    qq = jnp.arange(tq, dtype=I32)[None, :]
    dist = jnp.stack([d * LANE + qq - kk for d in range(3)])
    onehot = (_t5_bucket(dist)[..., None] == jnp.arange(REL_BUCKETS, dtype=I32)).astype(F32)
    bias = jnp.einsum("dkqb,bh->dhkq", onehot, rel_bias.astype(F32), precision=lax.Precision.HIGHEST)
    qw = ATT_HEADS * ATT_DH
    row = lambda b, i: b * nq + i
    iq_spec = lambda p: pl.BlockSpec((tq, 2 * LANE), lambda b, i: (row(b, i), OFF_IQ // (2 * LANE) + p))
    kspec = pl.BlockSpec((seq, LANE), lambda b, i: (b, 0))
    return pl.pallas_call(
        functools.partial(_att_kernel, topk=topk, tq=tq, tki=tki),
        out_shape=jax.ShapeDtypeStruct((m, qw), BF16),
        grid=(bsz, nq),
        in_specs=[
            pl.BlockSpec((tq, qw), lambda b, i: (row(b, i), OFF_AQ // qw)),
            iq_spec(0), iq_spec(1), iq_spec(2), iq_spec(3),
            pl.BlockSpec((tq, LANE), lambda b, i: (row(b, i), OFF_IK // LANE)),
            kspec,
            pl.BlockSpec((LANE, seq), lambda b, i: (0, b)),
            kspec,
            pl.BlockSpec((1, ATT_DH), lambda b, i: (0, 0)),
            pl.BlockSpec((3, ATT_HEADS, LANE, tq), lambda b, i: (0, 0, 0, 0)),
        ],
        out_specs=pl.BlockSpec((tq, qw), lambda b, i: (row(b, i), 0)),
        scratch_shapes=[pltpu.VMEM((seq, tq), I32),
                        pltpu.VMEM((ATT_HEADS, ATT_DH, tq), BF16),
                        pltpu.VMEM((ATT_HEADS, ATT_DH, tq), F32)],
        compiler_params=_cparams(("arbitrary", "arbitrary")),
        name="sparse_attention",
    )(proj, proj, proj, proj, proj, proj, kn, vt, ikb, att_q_g.reshape(1, ATT_DH), bias)


MQ_WIN = 3 * LANE


def _mem_kernel(w0_ref, w1_ref, w2_ref, kv_ref, qg_ref, kg_ref, o_ref):
    win = jnp.concatenate([w0_ref[...], w1_ref[...], w2_ref[...]], axis=1)
    mq = win[:, MQ_SHIFT:MQ_SHIFT + MEM_HEADS * MEM_DH]
    qg = qg_ref[...]
    kg = kg_ref[...]
    scale = MEM_DH ** -0.5
    kvw = MEM_HEADS * MEM_DH
    for h in range(MEM_HEADS):
        q = (_rms(mq[:, h * MEM_DH:(h + 1) * MEM_DH]) * qg).astype(BF16)
        k = (_rms(kv_ref[:, h * MEM_DH:(h + 1) * MEM_DH]) * kg).astype(BF16)
        v = kv_ref[:, kvw + h * MEM_DH:kvw + (h + 1) * MEM_DH].astype(BF16)
        lg = _dot_nt(q, k) * scale
        p = jnp.exp(lg - jnp.max(lg, axis=-1, keepdims=True))
        p = p / jnp.sum(p, axis=-1, keepdims=True)
        o_ref[:, h * MEM_DH:(h + 1) * MEM_DH] = _dot(p.astype(BF16), v).astype(o_ref.dtype)


def _memory_attention(proj, kv, mem_q_g, mem_k_g, bsz, seq, ts):
    ts = min(ts, seq)
    ns = seq // ts
    m = bsz * seq
    qw = MEM_HEADS * MEM_DH
    row = lambda b, i: b * ns + i
    wspec = lambda p: pl.BlockSpec((ts, MQ_WIN), lambda b, i: (row(b, i), OFF_IK // MQ_WIN + p))
    return pl.pallas_call(
        _mem_kernel,
        out_shape=jax.ShapeDtypeStruct((m, qw), BF16),
        grid=(bsz, ns),
        in_specs=[wspec(0), wspec(1), wspec(2),
                  pl.BlockSpec((MEM_LEN, 2 * qw), lambda b, i: (b, 0)),
                  pl.BlockSpec((1, MEM_DH), lambda b, i: (0, 0)),
                  pl.BlockSpec((1, MEM_DH), lambda b, i: (0, 0))],
        out_specs=pl.BlockSpec((ts, qw), lambda b, i: (row(b, i), 0)),
        compiler_params=_cparams(("arbitrary", "arbitrary")),
        name="memory_attention",
    )(proj, proj, proj, kv, mem_q_g.reshape(1, MEM_DH), mem_k_g.reshape(1, MEM_DH))


MERGE_TN = 256


def _merge_kernel(yr_ref, ya_ref, ym_ref, wr_ref, wa_ref, wm_ref,
                  g0_ref, e0_ref, g1_ref, e1_ref, g2_ref, e2_ref, o_ref):
    acc = None
    for y_ref, w_ref, g_ref, e_ref in ((yr_ref, wr_ref, g0_ref, e0_ref),
                                       (ya_ref, wa_ref, g1_ref, e1_ref),
                                       (ym_ref, wm_ref, g2_ref, e2_ref)):
        g = jnp.concatenate([g_ref[...], e_ref[...]], axis=1)[:, GATE_SHIFT:GATE_SHIFT + MERGE_TN]
        y = _dot(y_ref[...], w_ref[...].astype(BF16))
        term = y / (1.0 + jnp.exp(-g))
        acc = term if acc is None else acc + term
    o_ref[...] = acc.astype(o_ref.dtype)


def _merge(proj, y_ret, y_att, y_mem, w_up_ret, w_up_att, w_up_mem, d_model, tm):
    m = y_ret.shape[0]
    tm = min(tm, m)
    tn = MERGE_TN
    yspec = lambda y: pl.BlockSpec((tm, y.shape[1]), lambda i, n: (i, 0))
    wspec = lambda w: pl.BlockSpec((w.shape[0], tn), lambda i, n: (0, n))
    gspecs = []
    for br in range(3):
        base = GATE_BASE + br * d_model
        gspecs.append(pl.BlockSpec((tm, tn), lambda i, n, base=base: (i, base // tn + n)))
        gspecs.append(pl.BlockSpec((tm, LANE), lambda i, n, base=base: (i, base // LANE + (n + 1) * (tn // LANE))))
    return pl.pallas_call(
        _merge_kernel,
        out_shape=jax.ShapeDtypeStruct((m, d_model), BF16),
        grid=(m // tm, d_model // tn),
        in_specs=[yspec(y_ret), yspec(y_att), yspec(y_mem),
                  wspec(w_up_ret), wspec(w_up_att), wspec(w_up_mem)] + gspecs,
        out_specs=pl.BlockSpec((tm, tn), lambda i, n: (i, n)),
        compiler_params=_cparams(("arbitrary", "arbitrary")),
        name="gated_merge",
    )(y_ret, y_att, y_mem, w_up_ret, w_up_att, w_up_mem, *([proj] * 6))


RANK_NONE = 64.0


def _top_rows(vals, k, with_rank=False):
    rows = []
    rank = jnp.full(vals.shape, RANK_NONE, F32) if with_rank else None
    for r in range(k):
        mx = jnp.max(vals, axis=0, keepdims=True)
        rows.append(mx)
        top = vals == mx
        if with_rank:
            rank = jnp.where(top, r + 1.0, rank)
        vals = jnp.where(top, -jnp.inf, vals)
    return (rows, rank) if with_rank else rows


def _router_kernel(q_ref, keys_ref, c0_ref, e0_ref, r1_ref, e1_ref):
    kk = PEER_TOPK
    s_t = []
    for c in range(2):
        qs = q_ref[:, c * LANE:(c + 1) * LANE].astype(BF16)
        s_t.append(_dot_nt(keys_ref[c].astype(BF16), qs))
    a = _top_rows(s_t[0], kk)
    b, rank1 = _top_rows(s_t[1], kk, with_rank=True)
    amat = jnp.concatenate(a, axis=0)
    bmat = jnp.concatenate(b, axis=0)
    half = kk // 2
    cand = jnp.concatenate([a[0] + bmat] + [a[r] + bmat[:half] for r in range(1, half)]
                           + [amat[half:] + b[0]], axis=0)
    tv = _top_rows(cand, kk)
    tau = tv[kk - 1]
    z = None
    for r in range(kk):
        e = jnp.exp(tv[r] - tv[0])
        z = e if z is None else z + e
    cnt = jnp.zeros(amat.shape, F32)
    for j in range(kk):
        cnt = cnt + jnp.where(amat + b[j] >= tau, 1.0, 0.0)
    count0 = jnp.zeros(s_t[0].shape, F32)
    for r in range(kk):
        count0 = jnp.where(s_t[0] == a[r], cnt[r:r + 1], count0)
    c0_ref[0] = count0
    e0_ref[0] = jnp.exp(s_t[0] - a[0]) * (1.0 / z)
    r1_ref[0] = rank1
    e1_ref[0] = jnp.exp(s_t[1] - b[0])


def _peer_router(q, sub_keys, tt):
    m = q.shape[0]
    tt = min(tt, m)
    nk = PEER_NKEYS
    keys = sub_keys.reshape(PEER_HEADS * 2, nk, PEER_DQ // 2)
    f_shape = jax.ShapeDtypeStruct((PEER_HEADS, nk, m), F32)
    s_spec = pl.BlockSpec((1, nk, tt), lambda i, h: (h, 0, i))
    return pl.pallas_call(
        _router_kernel,
        out_shape=(f_shape,) * 4,
        grid=(m // tt, PEER_HEADS),
        in_specs=[pl.BlockSpec((tt, PEER_DQ), lambda i, h: (i, h)),
                  pl.BlockSpec((2, nk, PEER_DQ // 2), lambda i, h: (h, 0, 0))],
        out_specs=(s_spec,) * 4,
        compiler_params=_cparams(("arbitrary", "arbitrary")),
        name="peer_router",
    )(q, keys)


def _gelu(x):
    return 0.5 * x * (1.0 + lax.erf(x * (2.0 ** -0.5)))


def _peer_gate_kernel(ht_ref, u_ref, c0_ref, e0_ref, r1_ref, e1_ref, o_ref, z_ref, *, te):
    e = pl.program_id(1)
    nsub = te // PEER_NKEYS
    tm = ht_ref.shape[1]
    z_ref[...] = _dot(u_ref[...], ht_ref[...])
    for ii in range(nsub):
        i_idx = e * nsub + ii
        rows = slice(ii * PEER_NKEYS, (ii + 1) * PEER_NKEYS)
        c0_rows = [c0_ref[h, pl.ds(i_idx, 1), :] for h in range(PEER_HEADS)]
        e0_rows = [e0_ref[h, pl.ds(i_idx, 1), :] for h in range(PEER_HEADS)]
        for tc in range(tm // LANE):
            cols = slice(tc * LANE, (tc + 1) * LANE)
            w = None
            for h in range(PEER_HEADS):
                hit = r1_ref[h, :, cols] <= c0_rows[h][:, cols]
                term = jnp.where(hit, e1_ref[h, :, cols] * e0_rows[h][:, cols], 0.0)
                w = term if w is None else w + term
            p_t = _gelu(z_ref[rows, cols]) * w
            o_ref[cols, rows] = p_t.T.astype(o_ref.dtype)


def _peer_gate(h2t, u_bf, c0t, e0t, r1t, e1t, tm, te):
    d, m = h2t.shape
    ne = u_bf.shape[0]
    tm = min(tm, m)
    sspec = pl.BlockSpec((PEER_HEADS, PEER_NKEYS, tm), lambda i, e: (0, 0, i))
    return pl.pallas_call(
        functools.partial(_peer_gate_kernel, te=te),
        out_shape=jax.ShapeDtypeStruct((m, ne), BF16),
        grid=(m // tm, ne // te),
        in_specs=[pl.BlockSpec((d, tm), lambda i, e: (0, i)),
                  pl.BlockSpec((te, d), lambda i, e: (e, 0)),
                  sspec, sspec, sspec, sspec],
        out_specs=pl.BlockSpec((tm, te), lambda i, e: (i, e)),
        scratch_shapes=[pltpu.VMEM((te, tm), F32)],
        compiler_params=_cparams(("arbitrary", "arbitrary")),
        name="peer_gate",
    )(h2t, u_bf, c0t, e0t, r1t, e1t)


def _peer_out_kernel(p_ref, v_ref, x_ref, o_ref):
    @pl.when(pl.program_id(2) == 0)
    def _():
        o_ref[...] = x_ref[...]

    o_ref[...] += _dot(p_ref[...], v_ref[...])


def _peer_out(p, v_bf, x, tm, tn, tk):
    m, ne = p.shape
    d = v_bf.shape[1]
    tm, tn, tk = min(tm, m), min(tn, d), min(tk, ne)
    return pl.pallas_call(
        _peer_out_kernel,
        out_shape=jax.ShapeDtypeStruct((m, d), F32),
        grid=(m // tm, d // tn, ne // tk),
        in_specs=[pl.BlockSpec((tm, tk), lambda i, n, k: (i, k)),
                  pl.BlockSpec((tk, tn), lambda i, n, k: (k, n)),
                  pl.BlockSpec((tm, tn), lambda i, n, k: (i, n))],
        out_specs=pl.BlockSpec((tm, tn), lambda i, n, k: (i, n)),
        compiler_params=_cparams(("arbitrary", "arbitrary", "arbitrary")),
        name="peer_out",
    )(p, v_bf, x)


def kernel(x, mem, positions, ln1_g, w_in, att_q_g, att_k_g, rel_bias, mem_g, w_mem_kv, mem_q_g, mem_k_g,
           w_up_ret, w_up_att, w_up_mem, w_out, ln2_g, peer_w_q, peer_sub_keys, peer_u, peer_v):
    bsz, seq, d = x.shape
    m = bsz * seq
    depth = ln1_g.shape[0]
    xf = x.reshape(m, d)
    memf = mem.reshape(bsz * mem.shape[1], d)
    for l in range(depth):
        h = _rmsnorm(xf, ln1_g[l], 256)
        proj = _matmul(h, jnp.swapaxes(w_in[l], 0, 1), tm=1024, tn=512, w_transposed=True, name="in_proj")
        y_ret = _retention(proj, positions, bsz, seq)
        kn, vt, ikb = _kprep(proj, att_k_g[l], 512)
        y_att = _sparse_attention(proj, kn, vt, ikb, att_q_g[l], rel_bias, bsz, seq)
        kv = _matmul(_rmsnorm(memf, mem_g[l], 256), w_mem_kv[l], tm=512, tn=512, name="mem_kv")
        y_mem = _memory_attention(proj, kv, mem_q_g[l], mem_k_g[l], bsz, seq, 512)
        merged = _merge(proj, y_ret, y_att, y_mem, w_up_ret[l], w_up_att[l], w_up_mem[l], d, 1024)
        xf = _matmul(merged, w_out[l], tm=1024, tn=512, residual=xf, name="out_proj")
        h2, h2t = _rmsnorm(xf, ln2_g[l], 256, with_transposed=True)
        q = _matmul(h2, peer_w_q[l], tm=1024, tn=512, name="peer_query")
        c0t, e0t, r1t, e1t = _peer_router(q, peer_sub_keys[l], 512)
        p = _peer_gate(h2t, peer_u[l].astype(BF16), c0t, e0t, r1t, e1t, 512, 1024)
        xf = _peer_out(p, peer_v[l].astype(BF16), xf, 512, 2048, 2048)
    return xf.reshape(bsz, seq, d)
```

```python
import functools
import math

import jax
import jax.numpy as jnp
from jax import lax
from jax.experimental import pallas as pl
from jax.experimental.pallas import tpu as pltpu

F32 = jnp.float32
BF16 = jnp.bfloat16
I32 = jnp.int32

MEM_LEN = 256
RET_HEADS, RET_DK, RET_DV, RET_CHUNK = 8, 128, 256, 128
ATT_HEADS, ATT_DH = 8, 128
IDX_HEADS, IDX_DH = 16, 64
TOPK_MAX = 256
MEM_HEADS, MEM_DH = 4, 256
REL_BUCKETS, REL_MAX_DIST = 32, 128
PEER_HEADS, PEER_NKEYS, PEER_DQ, PEER_TOPK = 8, 128, 256, 16
EPS = 1e-6
ROPE_BASE = 10000.0

OFF_RQ = 0
OFF_RK = OFF_RQ + RET_HEADS * RET_DK
OFF_RV = OFF_RK + RET_HEADS * RET_DK
OFF_RG = OFF_RV + RET_HEADS * RET_DV
OFF_AQ = OFF_RG + RET_HEADS * RET_DV
OFF_AK = OFF_AQ + ATT_HEADS * ATT_DH
OFF_AV = OFF_AK + ATT_DH
OFF_IQ = OFF_AV + ATT_DH
OFF_IK = OFF_IQ + IDX_HEADS * IDX_DH
OFF_IW = OFF_IK + IDX_DH
OFF_MQ = OFF_IW + IDX_HEADS
OFF_GATE = OFF_MQ + MEM_HEADS * MEM_DH

LANE = 128
GATE_SHIFT = OFF_GATE % LANE
GATE_BASE = OFF_GATE - GATE_SHIFT
MQ_SHIFT = OFF_MQ - OFF_IK
VMEM_LIMIT = 56 * 1024 * 1024

INT_MIN = -(2 ** 31)
NEG_BIG = -1e30


def _cparams(sem):
    return pltpu.CompilerParams(dimension_semantics=sem, vmem_limit_bytes=VMEM_LIMIT)


def _dot_nt(a, b):
    return lax.dot_general(a, b, (((1,), (1,)), ((), ())), preferred_element_type=F32)


def _dot(a, b):
    return jnp.dot(a, b, preferred_element_type=F32)


def _rms(x):
    return x * lax.rsqrt(jnp.mean(x * x, axis=-1, keepdims=True) + EPS)


def _rmsnorm_kernel(x_ref, g_ref, o_ref, *ot_ref):
    y = _rms(x_ref[...].astype(F32)) * g_ref[...]
    o_ref[...] = y.astype(o_ref.dtype)
    if ot_ref:
        ot_ref[0][...] = y.T.astype(o_ref.dtype)


def _rmsnorm(x, g, tm, with_transposed=False):
    m, d = x.shape
    tm = min(tm, m)
    out_shape = [jax.ShapeDtypeStruct((m, d), BF16)]
    out_specs = [pl.BlockSpec((tm, d), lambda i: (i, 0))]
    if with_transposed:
        out_shape.append(jax.ShapeDtypeStruct((d, m), BF16))
        out_specs.append(pl.BlockSpec((d, tm), lambda i: (0, i)))
    res = pl.pallas_call(
        _rmsnorm_kernel,
        out_shape=tuple(out_shape),
        grid=(m // tm,),
        in_specs=[pl.BlockSpec((tm, d), lambda i: (i, 0)),
                  pl.BlockSpec((1, d), lambda i: (0, 0))],
        out_specs=tuple(out_specs),
        compiler_params=_cparams(("arbitrary",)),
        name="rmsnorm",
    )(x, g.reshape(1, d))
    return res if with_transposed else res[0]


def _mm_kernel(*refs, has_res, w_transposed):
    if has_res:
        a_ref, w_ref, r_ref, o_ref, wsc = refs
    else:
        a_ref, w_ref, o_ref, wsc = refs

    @pl.when(pl.program_id(1) == 0)
    def _():
        wsc[...] = w_ref[...].astype(BF16)

    acc = _dot_nt(a_ref[...], wsc[...]) if w_transposed else _dot(a_ref[...], wsc[...])
    if has_res:
        acc = acc + r_ref[...]
    o_ref[...] = acc.astype(o_ref.dtype)


def _matmul(a, w, *, tm, tn, residual=None, out_dtype=F32, w_transposed=False, name="matmul"):
    m, k = a.shape
    n = w.shape[0] if w_transposed else w.shape[1]
    tm = min(tm, m)
    tn = min(tn, n)
    if w_transposed:
        w_spec = pl.BlockSpec((tn, k), lambda j, i: (j, 0))
        w_scratch = pltpu.VMEM((tn, k), BF16)
    else:
        w_spec = pl.BlockSpec((k, tn), lambda j, i: (0, j))
        w_scratch = pltpu.VMEM((k, tn), BF16)
    in_specs = [pl.BlockSpec((tm, k), lambda j, i: (i, 0)), w_spec]
    args = [a, w]
    if residual is not None:
        in_specs.append(pl.BlockSpec((tm, tn), lambda j, i: (i, j)))
        args.append(residual)
    return pl.pallas_call(
        functools.partial(_mm_kernel, has_res=residual is not None, w_transposed=w_transposed),
        out_shape=jax.ShapeDtypeStruct((m, n), out_dtype),
        grid=(pl.cdiv(n, tn), m // tm),
        in_specs=in_specs,
        out_specs=pl.BlockSpec((tm, tn), lambda j, i: (i, j)),
        scratch_shapes=[w_scratch],
        compiler_params=_cparams(("arbitrary", "arbitrary")),
        name=name,
    )(*args)


A_RING = 3


def _mm_ring_kernel(a_hbm, w_ref, o_ref, abuf, sem, wsc, *, tm, ni, nsteps):
    i = pl.program_id(1)
    s = pl.program_id(0) * ni + i

    def a_copy(step, slot):
        row = pl.multiple_of(lax.rem(step, ni) * tm, tm)
        return pltpu.make_async_copy(a_hbm.at[pl.ds(row, tm), :], abuf.at[slot], sem.at[slot])

    @pl.when(s == 0)
    def _():
        a_copy(0, 0).start()
        a_copy(1, 1).start()

    @pl.when(s + 2 < nsteps)
    def _():
        a_copy(s + 2, lax.rem(s + 2, A_RING)).start()

    @pl.when(i == 0)
    def _():
        wsc[...] = w_ref[...].astype(BF16)

    slot = lax.rem(s, A_RING)
    a_copy(s, slot).wait()
    o_ref[...] = _dot_nt(abuf[slot], wsc[...]).astype(o_ref.dtype)


def _matmul_ring(a, w_t, *, tm, tn, out_dtype=F32, name="matmul_ring"):
    m, k = a.shape
    n = w_t.shape[0]
    tm = min(tm, m)
    tn = min(tn, n)
    ni = m // tm
    nj = pl.cdiv(n, tn)
    assert ni * nj >= 2
    return pl.pallas_call(
        functools.partial(_mm_ring_kernel, tm=tm, ni=ni, nsteps=ni * nj),
        out_shape=jax.ShapeDtypeStruct((m, n), out_dtype),
        grid=(nj, ni),
        in_specs=[pl.BlockSpec(memory_space=pl.ANY),
                  pl.BlockSpec((tn, k), lambda j, i: (j, 0))],
        out_specs=pl.BlockSpec((tm, tn), lambda j, i: (i, j)),
        scratch_shapes=[pltpu.VMEM((A_RING, tm, k), BF16),
                        pltpu.SemaphoreType.DMA((A_RING,)),
                        pltpu.VMEM((tn, k), BF16)],
        compiler_params=_cparams(("arbitrary", "arbitrary")),
        name=name,
    )(a, w_t)


def _ret_kernel(cdec_ref, rq_ref, rk_ref, rv_ref, rg_ref, pos_ref, invf_ref, sign_ref,
                idec_ref, qdec_ref, kdec_ref, o_ref, st_ref):
    @pl.when(pl.program_id(1) == 0)
    def _():
        st_ref[...] = jnp.zeros_like(st_ref)

    ang = pos_ref[...].astype(F32) * invf_ref[...]
    cos = jnp.cos(ang)
    sin = jnp.sin(ang) * sign_ref[...]
    scale = RET_DK ** -0.5
    for h in range(RET_HEADS):
        q = rq_ref[:, h * RET_DK:(h + 1) * RET_DK]
        k = rk_ref[:, h * RET_DK:(h + 1) * RET_DK]
        q = q * cos + pltpu.roll(q, RET_DK // 2, 1) * sin
        k = (k * cos + pltpu.roll(k, RET_DK // 2, 1) * sin) * scale
        vb = rv_ref[:, h * RET_DV:(h + 1) * RET_DV].astype(BF16)
        s = _dot_nt(q.astype(BF16), k.astype(BF16)) * idec_ref[h]
        inner = _dot(s.astype(BF16), vb)
        st = st_ref[h]
        cross = _dot((q * qdec_ref[h]).astype(BF16), st.astype(BF16))
        kd = (k * kdec_ref[h]).astype(BF16)
        upd = lax.dot_general(kd, vb, (((0,), (0,)), ((), ())), preferred_element_type=F32)
        st_ref[h] = st * cdec_ref[h] + upd
        y = _rms(inner + cross)
        g = rg_ref[:, h * RET_DV:(h + 1) * RET_DV]
        o_ref[:, h * RET_DV:(h + 1) * RET_DV] = (g / (1.0 + jnp.exp(-g)) * y).astype(o_ref.dtype)


def _retention(proj, positions, bsz, seq):
    c = RET_CHUNK
    nc = seq // c
    m = bsz * seq
    hh = RET_HEADS
    log_gamma = jnp.log1p(-jnp.exp2(-5.0 - jnp.arange(hh, dtype=F32)))
    idx = jnp.arange(c, dtype=F32)
    diff = idx[:, None] - idx[None, :]
    idec = jnp.where(diff[None] >= 0,
                     jnp.exp(log_gamma[:, None, None] * jnp.maximum(diff, 0.0)[None]), 0.0)
    qdec = jnp.exp(log_gamma[:, None] * (idx + 1.0)[None])
    kdec = jnp.exp(log_gamma[:, None] * (c - 1.0 - idx)[None])
    cdec = jnp.exp(log_gamma * c)
    qdec_b = jnp.broadcast_to(qdec[:, :, None], (hh, c, RET_DK))
    kdec_b = jnp.broadcast_to(kdec[:, :, None], (hh, c, RET_DK))
    half = RET_DK // 2
    inv_freq = 1.0 / (ROPE_BASE ** (jnp.arange(half, dtype=F32) / half))
    invf = jnp.concatenate([inv_freq, inv_freq]).reshape(1, RET_DK)
    sign = jnp.concatenate([-jnp.ones((half,), F32), jnp.ones((half,), F32)]).reshape(1, RET_DK)
    qk_w = hh * RET_DK
    v_w = hh * RET_DV
    row = lambda b, i: b * nc + i
    const3 = lambda b, i: (0, 0, 0)
    return pl.pallas_call(
        _ret_kernel,
        out_shape=jax.ShapeDtypeStruct((m, v_w), BF16),
        grid=(bsz, nc),
        in_specs=[
            pl.BlockSpec(memory_space=pltpu.SMEM),
            pl.BlockSpec((c, qk_w), lambda b, i: (row(b, i), OFF_RQ // qk_w)),
            pl.BlockSpec((c, qk_w), lambda b, i: (row(b, i), OFF_RK // qk_w)),
            pl.BlockSpec((c, v_w), lambda b, i: (row(b, i), OFF_RV // v_w)),
            pl.BlockSpec((c, v_w), lambda b, i: (row(b, i), OFF_RG // v_w)),
            pl.BlockSpec((c, 1), lambda b, i: (row(b, i), 0)),
            pl.BlockSpec((1, RET_DK), lambda b, i: (0, 0)),
            pl.BlockSpec((1, RET_DK), lambda b, i: (0, 0)),
            pl.BlockSpec((hh, c, c), const3),
            pl.BlockSpec((hh, c, RET_DK), const3),
            pl.BlockSpec((hh, c, RET_DK), const3),
        ],
        out_specs=pl.BlockSpec((c, v_w), lambda b, i: (row(b, i), 0)),
        scratch_shapes=[pltpu.VMEM((hh, RET_DK, RET_DV), F32)],
        compiler_params=_cparams(("arbitrary", "arbitrary")),
        name="retention",
    )(cdec, proj, proj, proj, proj, positions.reshape(m, 1), invf, sign, idec, qdec_b, kdec_b)


def _kprep_kernel(ak_ref, av_ref, ikw_ref, kg_ref, kn_ref, vt_ref, ik_ref):
    kn_ref[...] = (_rms(ak_ref[...]) * kg_ref[...]).astype(BF16)
    vt_ref[...] = av_ref[...].T.astype(BF16)
    ik_ref[...] = ikw_ref[...].astype(BF16)


def _kprep(proj, att_k_g, tm):
    m = proj.shape[0]
    tm = min(tm, m)
    blk = lambda off: pl.BlockSpec((tm, LANE), lambda i: (i, off // LANE))
    nat = jax.ShapeDtypeStruct((m, LANE), BF16)
    nat_spec = pl.BlockSpec((tm, LANE), lambda i: (i, 0))
    return pl.pallas_call(
        _kprep_kernel,
        out_shape=(nat, jax.ShapeDtypeStruct((LANE, m), BF16), nat),
        grid=(m // tm,),
        in_specs=[blk(OFF_AK), blk(OFF_AV), blk(OFF_IK),
                  pl.BlockSpec((1, LANE), lambda i: (0, 0))],
        out_specs=(nat_spec, pl.BlockSpec((LANE, tm), lambda i: (0, i)), nat_spec),
        compiler_params=_cparams(("arbitrary",)),
        name="att_kprep",
    )(proj, proj, proj, att_k_g.reshape(1, ATT_DH))


IDX_TK = 512
M_INIT = -1e29


def _att_kernel(aq_ref, iq0_ref, iq1_ref, iq2_ref, iq3_ref, ikw_ref, kn_ref, vt_ref, ik_ref,
                qg_ref, bias_ref, o_ref, sc_ref, qt_ref, acc_ref, *, topk, tq, tki):
    qi = pl.program_id(1)
    t0 = qi * tq
    nkb = (t0 + tq + tki - 1) // tki
    col_t = t0 + lax.broadcasted_iota(I32, (1, tq), 1)
    sub = tki // LANE

    heads_t = []
    for ref in (iq0_ref, iq1_ref, iq2_ref, iq3_ref):
        for c in range(2):
            pair_t = ref[:, c * LANE:(c + 1) * LANE].T
            heads_t.append(pair_t[:IDX_DH])
            heads_t.append(pair_t[IDX_DH:])
    rhs_all = jnp.concatenate(heads_t, axis=1).astype(BF16)
    w_t = ikw_ref[...].T
    idx_scale = (IDX_DH ** -0.5) * (IDX_HEADS ** -0.5)

    def score_blk(kb, carry):
        k0 = pl.multiple_of(kb * tki, tki)
        ikb = ik_ref[pl.ds(k0, tki), :][:, :IDX_DH]
        lg = _dot(ikb, rhs_all)
        acc = None
        for h in range(IDX_HEADS):
            term = w_t[IDX_DH + h:IDX_DH + h + 1, :] * jnp.maximum(lg[:, h * tq:(h + 1) * tq], 0.0)
            acc = term if acc is None else acc + term
        acc = acc * idx_scale
        s_idx = k0 + lax.broadcasted_iota(I32, (tki, 1), 0)
        acc = jnp.where(s_idx <= col_t, acc, -jnp.inf)
        bits = pltpu.bitcast(acc, I32)
        sc_ref[pl.ds(k0, tki), :] = jnp.where(bits < 0, bits ^ 0x7FFFFFFF, bits)
        return carry

    lax.fori_loop(0, nkb, score_blk, 0)

    def count_ge(cand):
        def body(kb, cnt):
            k0 = pl.multiple_of(kb * tki, tki)
            c = jnp.where(sc_ref[pl.ds(k0, tki), :] >= cand, 1, 0)
            return cnt + jnp.sum(c.reshape(tki // 8, 8, tq), axis=0)
        cnt = lax.fori_loop(0, nkb, body, jnp.zeros((8, tq), I32))
        return jnp.sum(cnt, axis=0, keepdims=True)

    base = jnp.where(count_ge(jnp.zeros((1, tq), I32)) >= topk, 0, INT_MIN).astype(I32)

    def bit_body(i, base):
        cand = base | jnp.left_shift(jnp.int32(1), 30 - i)
        return jnp.where(count_ge(cand) >= topk, cand, base)

    tau = lax.fori_loop(0, 31, bit_body, base)

    qg = qg_ref[...]
    scale = ATT_DH ** -0.5
    nfar = jnp.maximum(qi - 1, 0) // sub

    for h in range(ATT_HEADS):
        qt_ref[h] = (_rms(aq_ref[:, h * ATT_DH:(h + 1) * ATT_DH]) * qg).T.astype(BF16)
    acc_ref[...] = jnp.zeros_like(acc_ref)

    def softmax_block(carry, k0, width, sel, bias_of):
        m_all, l_all = carry
        kblk = kn_ref[pl.ds(k0, width), :]
        vblk = vt_ref[:, pl.ds(k0, width)]
        m_rows, l_rows = [], []
        for h in range(ATT_HEADS):
            lg = jnp.where(sel, _dot(kblk, qt_ref[h]) * scale + bias_of(h), NEG_BIG)
            m_i = m_all[h:h + 1]
            m_new = jnp.maximum(m_i, jnp.max(lg, axis=0, keepdims=True))
            alpha = jnp.exp(m_i - m_new)
            p = jnp.exp(lg - m_new)
            m_rows.append(m_new)
            l_rows.append(alpha * l_all[h:h + 1] + jnp.sum(p, axis=0, keepdims=True))
            acc_ref[h] = alpha * acc_ref[h] + _dot(vblk, p.astype(BF16))
        return jnp.concatenate(m_rows, axis=0), jnp.concatenate(l_rows, axis=0)

    def far_blk(kb, carry):
        k0 = pl.multiple_of(kb * tki, tki)
        sel = sc_ref[pl.ds(k0, tki), :] >= tau
        return softmax_block(carry, k0, tki, sel,
                             lambda h: jnp.concatenate([bias_ref[2, h]] * sub, axis=0))

    def near_blk(j, carry):
        k0 = pl.multiple_of(j * LANE, LANE)
        s_idx = k0 + lax.broadcasted_iota(I32, (LANE, 1), 0)
        sel = jnp.logical_and(sc_ref[pl.ds(k0, LANE), :] >= tau, s_idx <= col_t)
        d = jnp.minimum(qi - j, 2)
        return softmax_block(carry, k0, LANE, sel, lambda h: bias_ref[d, h])

    init = (jnp.full((ATT_HEADS, tq), M_INIT, F32), jnp.zeros((ATT_HEADS, tq), F32))
    carry = lax.fori_loop(0, nfar, far_blk, init)
    _, l_f = lax.fori_loop(nfar * sub, qi + 1, near_blk, carry)
    for h in range(ATT_HEADS):
        o_ref[:, h * ATT_DH:(h + 1) * ATT_DH] = (acc_ref[h] / l_f[h:h + 1]).T.astype(o_ref.dtype)


def _t5_bucket(dist):
    n = jnp.maximum(dist, 0)
    max_exact = REL_BUCKETS // 2
    large = max_exact + (jnp.log(jnp.maximum(n, 1).astype(F32) / max_exact)
                         / math.log(REL_MAX_DIST / max_exact) * (REL_BUCKETS - max_exact)).astype(I32)
    large = jnp.minimum(large, REL_BUCKETS - 1)
    return jnp.where(n < max_exact, n, large)


def _sparse_attention(proj, kn, vt, ikb, att_q_g, rel_bias, bsz, seq):
    tq = LANE
    m = bsz * seq
    nq = seq // tq
    topk = min(TOPK_MAX, seq // 4)
    tki = min(IDX_TK, seq)
    kk = jnp.arange(LANE, dtype=I32)[:, None]
    qq = jnp.arange(tq, dtype=I32)[None, :]
    dist = jnp.stack([d * LANE + qq - kk for d in range(3)])
    onehot = (_t5_bucket(dist)[..., None] == jnp.arange(REL_BUCKETS, dtype=I32)).astype(F32)
    bias = jnp.einsum("dkqb,bh->dhkq", onehot, rel_bias.astype(F32), precision=lax.Precision.HIGHEST)
    qw = ATT_HEADS * ATT_DH
    row = lambda b, i: b * nq + i
    iq_spec = lambda p: pl.BlockSpec((tq, 2 * LANE), lambda b, i: (row(b, i), OFF_IQ // (2 * LANE) + p))
    kspec = pl.BlockSpec((seq, LANE), lambda b, i: (b, 0))
    return pl.pallas_call(
        functools.partial(_att_kernel, topk=topk, tq=tq, tki=tki),
        out_shape=jax.ShapeDtypeStruct((m, qw), BF16),
        grid=(bsz, nq),
        in_specs=[
            pl.BlockSpec((tq, qw), lambda b, i: (row(b, i), OFF_AQ // qw)),
            iq_spec(0), iq_spec(1), iq_spec(2), iq_spec(3),
            pl.BlockSpec((tq, LANE), lambda b, i: (row(b, i), OFF_IK // LANE)),
            kspec,
            pl.BlockSpec((LANE, seq), lambda b, i: (0, b)),
            kspec,
            pl.BlockSpec((1, ATT_DH), lambda b, i: (0, 0)),
            pl.BlockSpec((3, ATT_HEADS, LANE, tq), lambda b, i: (0, 0, 0, 0)),
        ],
        out_specs=pl.BlockSpec((tq, qw), lambda b, i: (row(b, i), 0)),
        scratch_shapes=[pltpu.VMEM((seq, tq), I32),
                        pltpu.VMEM((ATT_HEADS, ATT_DH, tq), BF16),
                        pltpu.VMEM((ATT_HEADS, ATT_DH, tq), F32)],
        compiler_params=_cparams(("arbitrary", "arbitrary")),
        name="sparse_attention",
    )(proj, proj, proj, proj, proj, proj, kn, vt, ikb, att_q_g.reshape(1, ATT_DH), bias)


MQ_WIN = 3 * LANE


def _mem_kernel(w0_ref, w1_ref, w2_ref, kv_ref, qg_ref, kg_ref, o_ref):
    win = jnp.concatenate([w0_ref[...], w1_ref[...], w2_ref[...]], axis=1)
    mq = win[:, MQ_SHIFT:MQ_SHIFT + MEM_HEADS * MEM_DH]
    qg = qg_ref[...]
    kg = kg_ref[...]
    scale = MEM_DH ** -0.5
    kvw = MEM_HEADS * MEM_DH
    for h in range(MEM_HEADS):
        q = (_rms(mq[:, h * MEM_DH:(h + 1) * MEM_DH]) * qg).astype(BF16)
        k = (_rms(kv_ref[:, h * MEM_DH:(h + 1) * MEM_DH]) * kg).astype(BF16)
        v = kv_ref[:, kvw + h * MEM_DH:kvw + (h + 1) * MEM_DH].astype(BF16)
        lg = _dot_nt(q, k) * scale
        p = jnp.exp(lg - jnp.max(lg, axis=-1, keepdims=True))
        p = p / jnp.sum(p, axis=-1, keepdims=True)
        o_ref[:, h * MEM_DH:(h + 1) * MEM_DH] = _dot(p.astype(BF16), v).astype(o_ref.dtype)


def _memory_attention(proj, kv, mem_q_g, mem_k_g, bsz, seq, ts):
    ts = min(ts, seq)
    ns = seq // ts
    m = bsz * seq
    qw = MEM_HEADS * MEM_DH
    row = lambda b, i: b * ns + i
    wspec = lambda p: pl.BlockSpec((ts, MQ_WIN), lambda b, i: (row(b, i), OFF_IK // MQ_WIN + p))
    return pl.pallas_call(
        _mem_kernel,
        out_shape=jax.ShapeDtypeStruct((m, qw), BF16),
        grid=(bsz, ns),
        in_specs=[wspec(0), wspec(1), wspec(2),
                  pl.BlockSpec((MEM_LEN, 2 * qw), lambda b, i: (b, 0)),
                  pl.BlockSpec((1, MEM_DH), lambda b, i: (0, 0)),
                  pl.BlockSpec((1, MEM_DH), lambda b, i: (0, 0))],
        out_specs=pl.BlockSpec((ts, qw), lambda b, i: (row(b, i), 0)),
        compiler_params=_cparams(("arbitrary", "arbitrary")),
        name="memory_attention",
    )(proj, proj, proj, kv, mem_q_g.reshape(1, MEM_DH), mem_k_g.reshape(1, MEM_DH))


MERGE_TN = 256


def _merge_kernel(yr_ref, ya_ref, ym_ref, wr_ref, wa_ref, wm_ref,
                  g0_ref, e0_ref, g1_ref, e1_ref, g2_ref, e2_ref, o_ref):
    acc = None
    for y_ref, w_ref, g_ref, e_ref in ((yr_ref, wr_ref, g0_ref, e0_ref),
                                       (ya_ref, wa_ref, g1_ref, e1_ref),
                                       (ym_ref, wm_ref, g2_ref, e2_ref)):
        g = jnp.concatenate([g_ref[...], e_ref[...]], axis=1)[:, GATE_SHIFT:GATE_SHIFT + MERGE_TN]
        y = _dot(y_ref[...], w_ref[...].astype(BF16))
        term = y / (1.0 + jnp.exp(-g))
        acc = term if acc is None else acc + term
    o_ref[...] = acc.astype(o_ref.dtype)


def _merge(proj, y_ret, y_att, y_mem, w_up_ret, w_up_att, w_up_mem, d_model, tm):
    m = y_ret.shape[0]
    tm = min(tm, m)
    tn = MERGE_TN
    yspec = lambda y: pl.BlockSpec((tm, y.shape[1]), lambda i, n: (i, 0))
    wspec = lambda w: pl.BlockSpec((w.shape[0], tn), lambda i, n: (0, n))
    gspecs = []
    for br in range(3):
        base = GATE_BASE + br * d_model
        gspecs.append(pl.BlockSpec((tm, tn), lambda i, n, base=base: (i, base // tn + n)))
        gspecs.append(pl.BlockSpec((tm, LANE), lambda i, n, base=base: (i, base // LANE + (n + 1) * (tn // LANE))))
    return pl.pallas_call(
        _merge_kernel,
        out_shape=jax.ShapeDtypeStruct((m, d_model), BF16),
        grid=(m // tm, d_model // tn),
        in_specs=[yspec(y_ret), yspec(y_att), yspec(y_mem),
                  wspec(w_up_ret), wspec(w_up_att), wspec(w_up_mem)] + gspecs,
        out_specs=pl.BlockSpec((tm, tn), lambda i, n: (i, n)),
        compiler_params=_cparams(("arbitrary", "arbitrary")),
        name="gated_merge",
    )(y_ret, y_att, y_mem, w_up_ret, w_up_att, w_up_mem, *([proj] * 6))


RANK_NONE = 64.0


def _top_rows(vals, k, with_rank=False):
    rows = []
    rank = jnp.full(vals.shape, RANK_NONE, F32) if with_rank else None
    for r in range(k):
        mx = jnp.max(vals, axis=0, keepdims=True)
        rows.append(mx)
        top = vals == mx
        if with_rank:
            rank = jnp.where(top, r + 1.0, rank)
        vals = jnp.where(top, -jnp.inf, vals)
    return (rows, rank) if with_rank else rows


def _router_kernel(q_ref, keys_ref, c0_ref, e0_ref, r1_ref, e1_ref):
    kk = PEER_TOPK
    s_t = []
    for c in range(2):
        qs = q_ref[:, c * LANE:(c + 1) * LANE].astype(BF16)
        s_t.append(_dot_nt(keys_ref[c].astype(BF16), qs))
    a = _top_rows(s_t[0], kk)
    b, rank1 = _top_rows(s_t[1], kk, with_rank=True)
    amat = jnp.concatenate(a, axis=0)
    bmat = jnp.concatenate(b, axis=0)
    half = kk // 2
    cand = jnp.concatenate([a[0] + bmat] + [a[r] + bmat[:half] for r in range(1, half)]
                           + [amat[half:] + b[0]], axis=0)
    tv = _top_rows(cand, kk)
    tau = tv[kk - 1]
    z = None
    for r in range(kk):
        e = jnp.exp(tv[r] - tv[0])
        z = e if z is None else z + e
    cnt = jnp.zeros(amat.shape, F32)
    for j in range(kk):
        cnt = cnt + jnp.where(amat + b[j] >= tau, 1.0, 0.0)
    count0 = jnp.zeros(s_t[0].shape, F32)
    for r in range(kk):
        count0 = jnp.where(s_t[0] == a[r], cnt[r:r + 1], count0)
    c0_ref[0] = count0
    e0_ref[0] = jnp.exp(s_t[0] - a[0]) * (1.0 / z)
    r1_ref[0] = rank1
    e1_ref[0] = jnp.exp(s_t[1] - b[0])


def _peer_router(q, sub_keys, tt):
    m = q.shape[0]
    tt = min(tt, m)
    nk = PEER_NKEYS
    keys = sub_keys.reshape(PEER_HEADS * 2, nk, PEER_DQ // 2)
    f_shape = jax.ShapeDtypeStruct((PEER_HEADS, nk, m), F32)
    s_spec = pl.BlockSpec((1, nk, tt), lambda i, h: (h, 0, i))
    return pl.pallas_call(
        _router_kernel,
        out_shape=(f_shape,) * 4,
        grid=(m // tt, PEER_HEADS),
        in_specs=[pl.BlockSpec((tt, PEER_DQ), lambda i, h: (i, h)),
                  pl.BlockSpec((2, nk, PEER_DQ // 2), lambda i, h: (h, 0, 0))],
        out_specs=(s_spec,) * 4,
        compiler_params=_cparams(("arbitrary", "arbitrary")),
        name="peer_router",
    )(q, keys)


def _gelu(x):
    return 0.5 * x * (1.0 + lax.erf(x * (2.0 ** -0.5)))


def _peer_gate_kernel(ht_ref, u_ref, c0_ref, e0_ref, r1_ref, e1_ref, o_ref, z_ref, *, te):
    e = pl.program_id(1)
    nsub = te // PEER_NKEYS
    tm = ht_ref.shape[1]
    z_ref[...] = _dot(u_ref[...], ht_ref[...])
    for ii in range(nsub):
        i_idx = e * nsub + ii
        rows = slice(ii * PEER_NKEYS, (ii + 1) * PEER_NKEYS)
        c0_rows = [c0_ref[h, pl.ds(i_idx, 1), :] for h in range(PEER_HEADS)]
        e0_rows = [e0_ref[h, pl.ds(i_idx, 1), :] for h in range(PEER_HEADS)]
        for tc in range(tm // LANE):
            cols = slice(tc * LANE, (tc + 1) * LANE)
            w = None
            for h in range(PEER_HEADS):
                hit = r1_ref[h, :, cols] <= c0_rows[h][:, cols]
                term = jnp.where(hit, e1_ref[h, :, cols] * e0_rows[h][:, cols], 0.0)
                w = term if w is None else w + term
            p_t = _gelu(z_ref[rows, cols]) * w
            o_ref[cols, rows] = p_t.T.astype(o_ref.dtype)


def _peer_gate(h2t, u_bf, c0t, e0t, r1t, e1t, tm, te):
    d, m = h2t.shape
    ne = u_bf.shape[0]
    tm = min(tm, m)
    sspec = pl.BlockSpec((PEER_HEADS, PEER_NKEYS, tm), lambda i, e: (0, 0, i))
    return pl.pallas_call(
        functools.partial(_peer_gate_kernel, te=te),
        out_shape=jax.ShapeDtypeStruct((m, ne), BF16),
        grid=(m // tm, ne // te),
        in_specs=[pl.BlockSpec((d, tm), lambda i, e: (0, i)),
                  pl.BlockSpec((te, d), lambda i, e: (e, 0)),
                  sspec, sspec, sspec, sspec],
        out_specs=pl.BlockSpec((tm, te), lambda i, e: (i, e)),
        scratch_shapes=[pltpu.VMEM((te, tm), F32)],
        compiler_params=_cparams(("arbitrary", "arbitrary")),
        name="peer_gate",
    )(h2t, u_bf, c0t, e0t, r1t, e1t)


def _peer_out_kernel(p_ref, v_ref, x_ref, o_ref):
    @pl.when(pl.program_id(2) == 0)
    def _():
        o_ref[...] = x_ref[...]

    o_ref[...] += _dot(p_ref[...], v_ref[...])


def _peer_out(p, v_bf, x, tm, tn, tk):
    m, ne = p.shape
    d = v_bf.shape[1]
    tm, tn, tk = min(tm, m), min(tn, d), min(tk, ne)
    return pl.pallas_call(
        _peer_out_kernel,
        out_shape=jax.ShapeDtypeStruct((m, d), F32),
        grid=(m // tm, d // tn, ne // tk),
        in_specs=[pl.BlockSpec((tm, tk), lambda i, n, k: (i, k)),
                  pl.BlockSpec((tk, tn), lambda i, n, k: (k, n)),
                  pl.BlockSpec((tm, tn), lambda i, n, k: (i, n))],
        out_specs=pl.BlockSpec((tm, tn), lambda i, n, k: (i, n)),
        compiler_params=_cparams(("arbitrary", "arbitrary", "arbitrary")),
        name="peer_out",
    )(p, v_bf, x)


def kernel(x, mem, positions, ln1_g, w_in, att_q_g, att_k_g, rel_bias, mem_g, w_mem_kv, mem_q_g, mem_k_g,
           w_up_ret, w_up_att, w_up_mem, w_out, ln2_g, peer_w_q, peer_sub_keys, peer_u, peer_v):
    bsz, seq, d = x.shape
    m = bsz * seq
    depth = ln1_g.shape[0]
    xf = x.reshape(m, d)
    memf = mem.reshape(bsz * mem.shape[1], d)
    for l in range(depth):
        h = _rmsnorm(xf, ln1_g[l], 256)
        proj = _matmul_ring(h, jnp.swapaxes(w_in[l], 0, 1), tm=1024, tn=512, name="in_proj")
        y_ret = _retention(proj, positions, bsz, seq)
        kn, vt, ikb = _kprep(proj, att_k_g[l], 512)
        y_att = _sparse_attention(proj, kn, vt, ikb, att_q_g[l], rel_bias, bsz, seq)
        kv = _matmul(_rmsnorm(memf, mem_g[l], 256), w_mem_kv[l], tm=512, tn=512, name="mem_kv")
        y_mem = _memory_attention(proj, kv, mem_q_g[l], mem_k_g[l], bsz, seq, 512)
        merged = _merge(proj, y_ret, y_att, y_mem, w_up_ret[l], w_up_att[l], w_up_mem[l], d, 1024)
        xf = _matmul(merged, w_out[l], tm=1024, tn=512, residual=xf, name="out_proj")
        h2, h2t = _rmsnorm(xf, ln2_g[l], 256, with_transposed=True)
        q = _matmul(h2, peer_w_q[l], tm=1024, tn=512, name="peer_query")
        c0t, e0t, r1t, e1t = _peer_router(q, peer_sub_keys[l], 512)
        p = _peer_gate(h2t, peer_u[l].astype(BF16), c0t, e0t, r1t, e1t, 512, 1024)
        xf = _peer_out(p, peer_v[l].astype(BF16), xf, 512, 2048, 2048)
    return xf.reshape(bsz, seq, d)
```

```python
import functools
import math

import jax
import jax.numpy as jnp
from jax import lax
from jax.experimental import pallas as pl
from jax.experimental.pallas import tpu as pltpu

F32 = jnp.float32
BF16 = jnp.bfloat16
I32 = jnp.int32

MEM_LEN = 256
RET_HEADS, RET_DK, RET_DV, RET_CHUNK = 8, 128, 256, 128
ATT_HEADS, ATT_DH = 8, 128
IDX_HEADS, IDX_DH = 16, 64
TOPK_MAX = 256
MEM_HEADS, MEM_DH = 4, 256
REL_BUCKETS, REL_MAX_DIST = 32, 128
PEER_HEADS, PEER_NKEYS, PEER_DQ, PEER_TOPK = 8, 128, 256, 16
EPS = 1e-6
ROPE_BASE = 10000.0

OFF_RQ = 0
OFF_RK = OFF_RQ + RET_HEADS * RET_DK
OFF_RV = OFF_RK + RET_HEADS * RET_DK
OFF_RG = OFF_RV + RET_HEADS * RET_DV
OFF_AQ = OFF_RG + RET_HEADS * RET_DV
OFF_AK = OFF_AQ + ATT_HEADS * ATT_DH
OFF_AV = OFF_AK + ATT_DH
OFF_IQ = OFF_AV + ATT_DH
OFF_IK = OFF_IQ + IDX_HEADS * IDX_DH
OFF_IW = OFF_IK + IDX_DH
OFF_MQ = OFF_IW + IDX_HEADS
OFF_GATE = OFF_MQ + MEM_HEADS * MEM_DH

LANE = 128
GATE_SHIFT = OFF_GATE % LANE
GATE_BASE = OFF_GATE - GATE_SHIFT
MQ_SHIFT = OFF_MQ - OFF_IK
VMEM_LIMIT = 56 * 1024 * 1024

INT_MIN = -(2 ** 31)
NEG_BIG = -1e30


def _cparams(sem):
    return pltpu.CompilerParams(dimension_semantics=sem, vmem_limit_bytes=VMEM_LIMIT)


def _dot_nt(a, b):
    return lax.dot_general(a, b, (((1,), (1,)), ((), ())), preferred_element_type=F32)


def _dot(a, b):
    return jnp.dot(a, b, preferred_element_type=F32)


def _rms(x):
    return x * lax.rsqrt(jnp.mean(x * x, axis=-1, keepdims=True) + EPS)


def _rmsnorm_kernel(x_ref, g_ref, o_ref):
    o_ref[...] = (_rms(x_ref[...].astype(F32)) * g_ref[...]).astype(o_ref.dtype)


def _rmsnorm(x, g, tm):
    m, d = x.shape
    tm = min(tm, m)
    return pl.pallas_call(
        _rmsnorm_kernel,
        out_shape=jax.ShapeDtypeStruct((m, d), BF16),
        grid=(m // tm,),
        in_specs=[pl.BlockSpec((tm, d), lambda i: (i, 0)),
                  pl.BlockSpec((1, d), lambda i: (0, 0))],
        out_specs=pl.BlockSpec((tm, d), lambda i: (i, 0)),
        compiler_params=_cparams(("arbitrary",)),
        name="rmsnorm",
    )(x, g.reshape(1, d))


def _mm_kernel(*refs, has_res, w_transposed):
    if has_res:
        a_ref, w_ref, r_ref, o_ref, wsc = refs
    else:
        a_ref, w_ref, o_ref, wsc = refs

    @pl.when(pl.program_id(1) == 0)
    def _():
        wsc[...] = w_ref[...].astype(BF16)

    acc = _dot_nt(a_ref[...], wsc[...]) if w_transposed else _dot(a_ref[...], wsc[...])
    if has_res:
        acc = acc + r_ref[...]
    o_ref[...] = acc.astype(o_ref.dtype)


def _matmul(a, w, *, tm, tn, residual=None, out_dtype=F32, w_transposed=False, name="matmul"):
    m, k = a.shape
    n = w.shape[0] if w_transposed else w.shape[1]
    tm = min(tm, m)
    tn = min(tn, n)
    if w_transposed:
        w_spec = pl.BlockSpec((tn, k), lambda j, i: (j, 0))
        w_scratch = pltpu.VMEM((tn, k), BF16)
    else:
        w_spec = pl.BlockSpec((k, tn), lambda j, i: (0, j))
        w_scratch = pltpu.VMEM((k, tn), BF16)
    in_specs = [pl.BlockSpec((tm, k), lambda j, i: (i, 0)), w_spec]
    args = [a, w]
    if residual is not None:
        in_specs.append(pl.BlockSpec((tm, tn), lambda j, i: (i, j)))
        args.append(residual)
    return pl.pallas_call(
        functools.partial(_mm_kernel, has_res=residual is not None, w_transposed=w_transposed),
        out_shape=jax.ShapeDtypeStruct((m, n), out_dtype),
        grid=(pl.cdiv(n, tn), m // tm),
        in_specs=in_specs,
        out_specs=pl.BlockSpec((tm, tn), lambda j, i: (i, j)),
        scratch_shapes=[w_scratch],
        compiler_params=_cparams(("arbitrary", "arbitrary")),
        name=name,
    )(*args)


A_RING = 3


def _mm_ring_kernel(*refs, has_res, w_transposed, tm, ni, nsteps):
    if has_res:
        a_hbm, w_ref, r_ref, o_ref, abuf, sem, wsc = refs
    else:
        a_hbm, w_ref, o_ref, abuf, sem, wsc = refs
    i = pl.program_id(1)
    s = pl.program_id(0) * ni + i

    def a_copy(step, slot):
        row = pl.multiple_of(lax.rem(step, ni) * tm, tm)
        return pltpu.make_async_copy(a_hbm.at[pl.ds(row, tm), :], abuf.at[slot], sem.at[slot])

    @pl.when(s == 0)
    def _():
        a_copy(0, 0).start()
        a_copy(1, 1).start()

    @pl.when(s + 2 < nsteps)
    def _():
        a_copy(s + 2, lax.rem(s + 2, A_RING)).start()

    @pl.when(i == 0)
    def _():
        wsc[...] = w_ref[...].astype(BF16)

    slot = lax.rem(s, A_RING)
    a_copy(s, slot).wait()
    acc = _dot_nt(abuf[slot], wsc[...]) if w_transposed else _dot(abuf[slot], wsc[...])
    if has_res:
        acc = acc + r_ref[...]
    o_ref[...] = acc.astype(o_ref.dtype)


def _matmul_ring(a, w, *, tm, tn, residual=None, out_dtype=F32, w_transposed=False, name="matmul_ring"):
    m, k = a.shape
    n = w.shape[0] if w_transposed else w.shape[1]
    tm = min(tm, m)
    tn = min(tn, n)
    ni = m // tm
    nj = pl.cdiv(n, tn)
    assert ni * nj >= 2
    w_block = (tn, k) if w_transposed else (k, tn)
    w_index = (lambda j, i: (j, 0)) if w_transposed else (lambda j, i: (0, j))
    in_specs = [pl.BlockSpec(memory_space=pl.ANY), pl.BlockSpec(w_block, w_index)]
    args = [a, w]
    if residual is not None:
        in_specs.append(pl.BlockSpec((tm, tn), lambda j, i: (i, j)))
        args.append(residual)
    return pl.pallas_call(
        functools.partial(_mm_ring_kernel, has_res=residual is not None, w_transposed=w_transposed,
                          tm=tm, ni=ni, nsteps=ni * nj),
        out_shape=jax.ShapeDtypeStruct((m, n), out_dtype),
        grid=(nj, ni),
        in_specs=in_specs,
        out_specs=pl.BlockSpec((tm, tn), lambda j, i: (i, j)),
        scratch_shapes=[pltpu.VMEM((A_RING, tm, k), BF16),
                        pltpu.SemaphoreType.DMA((A_RING,)),
                        pltpu.VMEM(w_block, BF16)],
        compiler_params=_cparams(("arbitrary", "arbitrary")),
        name=name,
    )(*args)


def _ret_kernel(cdec_ref, rq_ref, rk_ref, rv_ref, rg_ref, pos_ref, invf_ref, sign_ref,
                idec_ref, qdec_ref, kdec_ref, o_ref, st_ref):
    @pl.when(pl.program_id(1) == 0)
    def _():
        st_ref[...] = jnp.zeros_like(st_ref)

    ang = pos_ref[...].astype(F32) * invf_ref[...]
    cos = jnp.cos(ang)
    sin = jnp.sin(ang) * sign_ref[...]
    scale = RET_DK ** -0.5
    for h in range(RET_HEADS):
        q = rq_ref[:, h * RET_DK:(h + 1) * RET_DK]
        k = rk_ref[:, h * RET_DK:(h + 1) * RET_DK]
        q = q * cos + pltpu.roll(q, RET_DK // 2, 1) * sin
        k = (k * cos + pltpu.roll(k, RET_DK // 2, 1) * sin) * scale
        vb = rv_ref[:, h * RET_DV:(h + 1) * RET_DV].astype(BF16)
        s = _dot_nt(q.astype(BF16), k.astype(BF16)) * idec_ref[h]
        inner = _dot(s.astype(BF16), vb)
        st = st_ref[h]
        cross = _dot((q * qdec_ref[h]).astype(BF16), st.astype(BF16))
        kd = (k * kdec_ref[h]).astype(BF16)
        upd = lax.dot_general(kd, vb, (((0,), (0,)), ((), ())), preferred_element_type=F32)
        st_ref[h] = st * cdec_ref[h] + upd
        y = _rms(inner + cross)
        g = rg_ref[:, h * RET_DV:(h + 1) * RET_DV]
        o_ref[:, h * RET_DV:(h + 1) * RET_DV] = (g / (1.0 + jnp.exp(-g)) * y).astype(o_ref.dtype)


def _retention(proj, positions, bsz, seq):
    c = RET_CHUNK
    nc = seq // c
    m = bsz * seq
    hh = RET_HEADS
    log_gamma = jnp.log1p(-jnp.exp2(-5.0 - jnp.arange(hh, dtype=F32)))
    idx = jnp.arange(c, dtype=F32)
    diff = idx[:, None] - idx[None, :]
    idec = jnp.where(diff[None] >= 0,
                     jnp.exp(log_gamma[:, None, None] * jnp.maximum(diff, 0.0)[None]), 0.0)
    qdec = jnp.exp(log_gamma[:, None] * (idx + 1.0)[None])
    kdec = jnp.exp(log_gamma[:, None] * (c - 1.0 - idx)[None])
    cdec = jnp.exp(log_gamma * c)
    qdec_b = jnp.broadcast_to(qdec[:, :, None], (hh, c, RET_DK))
    kdec_b = jnp.broadcast_to(kdec[:, :, None], (hh, c, RET_DK))
    half = RET_DK // 2
    inv_freq = 1.0 / (ROPE_BASE ** (jnp.arange(half, dtype=F32) / half))
    invf = jnp.concatenate([inv_freq, inv_freq]).reshape(1, RET_DK)
    sign = jnp.concatenate([-jnp.ones((half,), F32), jnp.ones((half,), F32)]).reshape(1, RET_DK)
    qk_w = hh * RET_DK
    v_w = hh * RET_DV
    row = lambda b, i: b * nc + i
    const3 = lambda b, i: (0, 0, 0)
    return pl.pallas_call(
        _ret_kernel,
        out_shape=jax.ShapeDtypeStruct((m, v_w), BF16),
        grid=(bsz, nc),
        in_specs=[
            pl.BlockSpec(memory_space=pltpu.SMEM),
            pl.BlockSpec((c, qk_w), lambda b, i: (row(b, i), OFF_RQ // qk_w)),
            pl.BlockSpec((c, qk_w), lambda b, i: (row(b, i), OFF_RK // qk_w)),
            pl.BlockSpec((c, v_w), lambda b, i: (row(b, i), OFF_RV // v_w)),
            pl.BlockSpec((c, v_w), lambda b, i: (row(b, i), OFF_RG // v_w)),
            pl.BlockSpec((c, 1), lambda b, i: (row(b, i), 0)),
            pl.BlockSpec((1, RET_DK), lambda b, i: (0, 0)),
            pl.BlockSpec((1, RET_DK), lambda b, i: (0, 0)),
            pl.BlockSpec((hh, c, c), const3),
            pl.BlockSpec((hh, c, RET_DK), const3),
            pl.BlockSpec((hh, c, RET_DK), const3),
        ],
        out_specs=pl.BlockSpec((c, v_w), lambda b, i: (row(b, i), 0)),
        scratch_shapes=[pltpu.VMEM((hh, RET_DK, RET_DV), F32)],
        compiler_params=_cparams(("arbitrary", "arbitrary")),
        name="retention",
    )(cdec, proj, proj, proj, proj, positions.reshape(m, 1), invf, sign, idec, qdec_b, kdec_b)


def _kprep_kernel(ak_ref, av_ref, ikw_ref, kg_ref, kn_ref, vt_ref, ik_ref):
    kn_ref[...] = (_rms(ak_ref[...]) * kg_ref[...]).astype(BF16)
    vt_ref[...] = av_ref[...].T.astype(BF16)
    ik_ref[...] = ikw_ref[...].astype(BF16)


def _kprep(proj, att_k_g, tm):
    m = proj.shape[0]
    tm = min(tm, m)
    blk = lambda off: pl.BlockSpec((tm, LANE), lambda i: (i, off // LANE))
    nat = jax.ShapeDtypeStruct((m, LANE), BF16)
    nat_spec = pl.BlockSpec((tm, LANE), lambda i: (i, 0))
    return pl.pallas_call(
        _kprep_kernel,
        out_shape=(nat, jax.ShapeDtypeStruct((LANE, m), BF16), nat),
        grid=(m // tm,),
        in_specs=[blk(OFF_AK), blk(OFF_AV), blk(OFF_IK),
                  pl.BlockSpec((1, LANE), lambda i: (0, 0))],
        out_specs=(nat_spec, pl.BlockSpec((LANE, tm), lambda i: (0, i)), nat_spec),
        compiler_params=_cparams(("arbitrary",)),
        name="att_kprep",
    )(proj, proj, proj, att_k_g.reshape(1, ATT_DH))


IDX_TK = 512
FAR_TK = 512
M_INIT = -1e29


def _att_kernel(aq_ref, iq0_ref, iq1_ref, iq2_ref, iq3_ref, ikw_ref, kn_ref, vt_ref, ik_ref,
                qg_ref, bias_ref, o_ref, sc_ref, qt_ref, acc_ref, *, topk, tq, tki):
    qi = pl.program_id(1)
    t0 = qi * tq
    nkb = (t0 + tq + tki - 1) // tki
    col_t = t0 + lax.broadcasted_iota(I32, (1, tq), 1)
    sub = tki // LANE

    heads_t = []
    for ref in (iq0_ref, iq1_ref, iq2_ref, iq3_ref):
        for c in range(2):
            pair_t = ref[:, c * LANE:(c + 1) * LANE].T
            heads_t.append(pair_t[:IDX_DH])
            heads_t.append(pair_t[IDX_DH:])
    rhs_all = jnp.concatenate(heads_t, axis=1).astype(BF16)
    w_t = ikw_ref[...].T
    idx_scale = (IDX_DH ** -0.5) * (IDX_HEADS ** -0.5)

    def score_blk(kb, carry):
        k0 = pl.multiple_of(kb * tki, tki)
        ikb = ik_ref[pl.ds(k0, tki), :][:, :IDX_DH]
        lg = _dot(ikb, rhs_all)
        acc = None
        for h in range(IDX_HEADS):
            term = w_t[IDX_DH + h:IDX_DH + h + 1, :] * jnp.maximum(lg[:, h * tq:(h + 1) * tq], 0.0)
            acc = term if acc is None else acc + term
        acc = acc * idx_scale
        s_idx = k0 + lax.broadcasted_iota(I32, (tki, 1), 0)
        acc = jnp.where(s_idx <= col_t, acc, -jnp.inf)
        bits = pltpu.bitcast(acc, I32)
        sc_ref[pl.ds(k0, tki), :] = jnp.where(bits < 0, bits ^ 0x7FFFFFFF, bits)
        return carry

    lax.fori_loop(0, nkb, score_blk, 0)

    def count_ge(cand):
        def body(kb, cnt):
            k0 = pl.multiple_of(kb * tki, tki)
            c = jnp.where(sc_ref[pl.ds(k0, tki), :] >= cand, 1, 0)
            return cnt + jnp.sum(c.reshape(tki // 8, 8, tq), axis=0)
        cnt = lax.fori_loop(0, nkb, body, jnp.zeros((8, tq), I32))
        return jnp.sum(cnt, axis=0, keepdims=True)

    base = jnp.where(count_ge(jnp.zeros((1, tq), I32)) >= topk, 0, INT_MIN).astype(I32)

    def bit_body(i, base):
        cand = base | jnp.left_shift(jnp.int32(1), 30 - i)
        return jnp.where(count_ge(cand) >= topk, cand, base)

    tau = lax.fori_loop(0, 31, bit_body, base)

    qg = qg_ref[...]
    scale = ATT_DH ** -0.5
    tkf = min(FAR_TK, tki)
    fsub = tkf // LANE
    nfar = jnp.maximum(qi - 1, 0) // fsub

    for h in range(ATT_HEADS):
        qt_ref[h] = (_rms(aq_ref[:, h * ATT_DH:(h + 1) * ATT_DH]) * qg).T.astype(BF16)
    acc_ref[...] = jnp.zeros_like(acc_ref)

    def softmax_block(carry, k0, width, sel, bias_of):
        m_all, l_all = carry
        kblk = kn_ref[pl.ds(k0, width), :]
        vblk = vt_ref[:, pl.ds(k0, width)]
        m_rows, l_rows = [], []
        for h in range(ATT_HEADS):
            lg = jnp.where(sel, _dot(kblk, qt_ref[h]) * scale + bias_of(h), NEG_BIG)
            m_i = m_all[h:h + 1]
            m_new = jnp.maximum(m_i, jnp.max(lg, axis=0, keepdims=True))
            alpha = jnp.exp(m_i - m_new)
            p = jnp.exp(lg - m_new)
            m_rows.append(m_new)
            l_rows.append(alpha * l_all[h:h + 1] + jnp.sum(p, axis=0, keepdims=True))
            acc_ref[h] = alpha * acc_ref[h] + _dot(vblk, p.astype(BF16))
        return jnp.concatenate(m_rows, axis=0), jnp.concatenate(l_rows, axis=0)

    def far_blk(kb, carry):
        k0 = pl.multiple_of(kb * tkf, tkf)
        sel = sc_ref[pl.ds(k0, tkf), :] >= tau
        return softmax_block(carry, k0, tkf, sel,
                             lambda h: jnp.concatenate([bias_ref[2, h]] * fsub, axis=0))

    def near_blk(j, carry):
        k0 = pl.multiple_of(j * LANE, LANE)
        s_idx = k0 + lax.broadcasted_iota(I32, (LANE, 1), 0)
        sel = jnp.logical_and(sc_ref[pl.ds(k0, LANE), :] >= tau, s_idx <= col_t)
        d = jnp.minimum(qi - j, 2)
        return softmax_block(carry, k0, LANE, sel, lambda h: bias_ref[d, h])

    init = (jnp.full((ATT_HEADS, tq), M_INIT, F32), jnp.zeros((ATT_HEADS, tq), F32))
    carry = lax.fori_loop(0, nfar, far_blk, init)
    _, l_f = lax.fori_loop(nfar * fsub, qi + 1, near_blk, carry)
    for h in range(ATT_HEADS):
        o_ref[:, h * ATT_DH:(h + 1) * ATT_DH] = (acc_ref[h] / l_f[h:h + 1]).T.astype(o_ref.dtype)


def _t5_bucket(dist):
    n = jnp.maximum(dist, 0)
    max_exact = REL_BUCKETS // 2
    large = max_exact + (jnp.log(jnp.maximum(n, 1).astype(F32) / max_exact)
                         / math.log(REL_MAX_DIST / max_exact) * (REL_BUCKETS - max_exact)).astype(I32)
    large = jnp.minimum(large, REL_BUCKETS - 1)
    return jnp.where(n < max_exact, n, large)


def _sparse_attention(proj, kn, vt, ikb, att_q_g, rel_bias, bsz, seq):
    tq = LANE
    m = bsz * seq
    nq = seq // tq
    topk = min(TOPK_MAX, seq // 4)
    tki = min(IDX_TK, seq)
    kk = jnp.arange(LANE, dtype=I32)[:, None]
    qq = jnp.arange(tq, dtype=I32)[None, :]
    dist = jnp.stack([d * LANE + qq - kk for d in range(3)])
    onehot = (_t5_bucket(dist)[..., None] == jnp.arange(REL_BUCKETS, dtype=I32)).astype(F32)
    bias = jnp.einsum("dkqb,bh->dhkq", onehot, rel_bias.astype(F32), precision=lax.Precision.HIGHEST)
    qw = ATT_HEADS * ATT_DH
    row = lambda b, i: b * nq + i
    iq_spec = lambda p: pl.BlockSpec((tq, 2 * LANE), lambda b, i: (row(b, i), OFF_IQ // (2 * LANE) + p))
    kspec = pl.BlockSpec((seq, LANE), lambda b, i: (b, 0))
    return pl.pallas_call(
        functools.partial(_att_kernel, topk=topk, tq=tq, tki=tki),
        out_shape=jax.ShapeDtypeStruct((m, qw), BF16),
        grid=(bsz, nq),
        in_specs=[
            pl.BlockSpec((tq, qw), lambda b, i: (row(b, i), OFF_AQ // qw)),
            iq_spec(0), iq_spec(1), iq_spec(2), iq_spec(3),
            pl.BlockSpec((tq, LANE), lambda b, i: (row(b, i), OFF_IK // LANE)),
            kspec,
            pl.BlockSpec((LANE, seq), lambda b, i: (0, b)),
            kspec,
            pl.BlockSpec((1, ATT_DH), lambda b, i: (0, 0)),
            pl.BlockSpec((3, ATT_HEADS, LANE, tq), lambda b, i: (0, 0, 0, 0)),
        ],
        out_specs=pl.BlockSpec((tq, qw), lambda b, i: (row(b, i), 0)),
        scratch_shapes=[pltpu.VMEM((seq, tq), I32),
                        pltpu.VMEM((ATT_HEADS, ATT_DH, tq), BF16),
                        pltpu.VMEM((ATT_HEADS, ATT_DH, tq), F32)],
        compiler_params=_cparams(("arbitrary", "arbitrary")),
        name="sparse_attention",
    )(proj, proj, proj, proj, proj, proj, kn, vt, ikb, att_q_g.reshape(1, ATT_DH), bias)


MQ_WIN = 3 * LANE


def _mem_kernel(w0_ref, w1_ref, w2_ref, kv_ref, qg_ref, kg_ref, o_ref):
    win = jnp.concatenate([w0_ref[...], w1_ref[...], w2_ref[...]], axis=1)
    mq = win[:, MQ_SHIFT:MQ_SHIFT + MEM_HEADS * MEM_DH]
    qg = qg_ref[...]
    kg = kg_ref[...]
    scale = MEM_DH ** -0.5
    kvw = MEM_HEADS * MEM_DH
    for h in range(MEM_HEADS):
        q = (_rms(mq[:, h * MEM_DH:(h + 1) * MEM_DH]) * qg).astype(BF16)
        k = (_rms(kv_ref[:, h * MEM_DH:(h + 1) * MEM_DH]) * kg).astype(BF16)
        v = kv_ref[:, kvw + h * MEM_DH:kvw + (h + 1) * MEM_DH].astype(BF16)
        lg = _dot_nt(q, k) * scale
        p = jnp.exp(lg - jnp.max(lg, axis=-1, keepdims=True))
        p = p / jnp.sum(p, axis=-1, keepdims=True)
        o_ref[:, h * MEM_DH:(h + 1) * MEM_DH] = _dot(p.astype(BF16), v).astype(o_ref.dtype)


def _memory_attention(proj, kv, mem_q_g, mem_k_g, bsz, seq, ts):
    ts = min(ts, seq)
    ns = seq // ts
    m = bsz * seq
    qw = MEM_HEADS * MEM_DH
    row = lambda b, i: b * ns + i
    wspec = lambda p: pl.BlockSpec((ts, MQ_WIN), lambda b, i: (row(b, i), OFF_IK // MQ_WIN + p))
    return pl.pallas_call(
        _mem_kernel,
        out_shape=jax.ShapeDtypeStruct((m, qw), BF16),
        grid=(bsz, ns),
        in_specs=[wspec(0), wspec(1), wspec(2),
                  pl.BlockSpec((MEM_LEN, 2 * qw), lambda b, i: (b, 0)),
                  pl.BlockSpec((1, MEM_DH), lambda b, i: (0, 0)),
                  pl.BlockSpec((1, MEM_DH), lambda b, i: (0, 0))],
        out_specs=pl.BlockSpec((ts, qw), lambda b, i: (row(b, i), 0)),
        compiler_params=_cparams(("arbitrary", "arbitrary")),
        name="memory_attention",
    )(proj, proj, proj, kv, mem_q_g.reshape(1, MEM_DH), mem_k_g.reshape(1, MEM_DH))


MERGE_TN = 256


def _merge_kernel(yr_ref, ya_ref, ym_ref, wr_ref, wa_ref, wm_ref,
                  g0_ref, e0_ref, g1_ref, e1_ref, g2_ref, e2_ref, o_ref):
    acc = None
    for y_ref, w_ref, g_ref, e_ref in ((yr_ref, wr_ref, g0_ref, e0_ref),
                                       (ya_ref, wa_ref, g1_ref, e1_ref),
                                       (ym_ref, wm_ref, g2_ref, e2_ref)):
        g = jnp.concatenate([g_ref[...], e_ref[...]], axis=1)[:, GATE_SHIFT:GATE_SHIFT + MERGE_TN]
        y = _dot(y_ref[...], w_ref[...].astype(BF16))
        term = y / (1.0 + jnp.exp(-g))
        acc = term if acc is None else acc + term
    o_ref[...] = acc.astype(o_ref.dtype)


def _merge(proj, y_ret, y_att, y_mem, w_up_ret, w_up_att, w_up_mem, d_model, tm):
    m = y_ret.shape[0]
    tm = min(tm, m)
    tn = MERGE_TN
    yspec = lambda y: pl.BlockSpec((tm, y.shape[1]), lambda i, n: (i, 0), pipeline_mode=pl.Buffered(1))
    wspec = lambda w: pl.BlockSpec((w.shape[0], tn), lambda i, n: (0, n))
    gspecs = []
    for br in range(3):
        base = GATE_BASE + br * d_model
        gspecs.append(pl.BlockSpec((tm, tn), lambda i, n, base=base: (i, base // tn + n)))
        gspecs.append(pl.BlockSpec((tm, LANE), lambda i, n, base=base: (i, base // LANE + (n + 1) * (tn // LANE))))
    return pl.pallas_call(
        _merge_kernel,
        out_shape=jax.ShapeDtypeStruct((m, d_model), BF16),
        grid=(m // tm, d_model // tn),
        in_specs=[yspec(y_ret), yspec(y_att), yspec(y_mem),
                  wspec(w_up_ret), wspec(w_up_att), wspec(w_up_mem)] + gspecs,
        out_specs=pl.BlockSpec((tm, tn), lambda i, n: (i, n)),
        compiler_params=_cparams(("arbitrary", "arbitrary")),
        name="gated_merge",
    )(y_ret, y_att, y_mem, w_up_ret, w_up_att, w_up_mem, *([proj] * 6))


RANK_NONE = 64.0


def _top_rows(vals, k, with_rank=False):
    rows = []
    rank = jnp.full(vals.shape, RANK_NONE, F32) if with_rank else None
    for r in range(k):
        mx = jnp.max(vals, axis=0, keepdims=True)
        rows.append(mx)
        top = vals == mx
        if with_rank:
            rank = jnp.where(top, r + 1.0, rank)
        vals = jnp.where(top, -jnp.inf, vals)
    return (rows, rank) if with_rank else rows


def _router_kernel(q_ref, keys_ref, c0_ref, e0_ref, r1_ref, e1_ref):
    kk = PEER_TOPK
    s_t = []
    for c in range(2):
        qs = q_ref[:, c * LANE:(c + 1) * LANE].astype(BF16)
        s_t.append(_dot_nt(keys_ref[c].astype(BF16), qs))
    a = _top_rows(s_t[0], kk)
    b, rank1 = _top_rows(s_t[1], kk, with_rank=True)
    amat = jnp.concatenate(a, axis=0)
    bmat = jnp.concatenate(b, axis=0)
    half = kk // 2
    cand = jnp.concatenate([a[0] + bmat] + [a[r] + bmat[:half] for r in range(1, half)]
                           + [amat[half:] + b[0]], axis=0)
    tv = _top_rows(cand, kk)
    tau = tv[kk - 1]
    z = None
    for r in range(kk):
        e = jnp.exp(tv[r] - tv[0])
        z = e if z is None else z + e
    cnt = jnp.zeros(amat.shape, F32)
    for j in range(kk):
        cnt = cnt + jnp.where(amat + b[j] >= tau, 1.0, 0.0)
    count0 = jnp.zeros(s_t[0].shape, F32)
    for r in range(kk):
        count0 = jnp.where(s_t[0] == a[r], cnt[r:r + 1], count0)
    c0_ref[0] = count0
    e0_ref[0] = jnp.exp(s_t[0] - a[0]) * (1.0 / z)
    r1_ref[0] = rank1
    e1_ref[0] = jnp.exp(s_t[1] - b[0])


def _peer_router(q, sub_keys, tt):
    m = q.shape[0]
    tt = min(tt, m)
    nk = PEER_NKEYS
    keys = sub_keys.reshape(PEER_HEADS * 2, nk, PEER_DQ // 2)
    f_shape = jax.ShapeDtypeStruct((PEER_HEADS, nk, m), F32)
    s_spec = pl.BlockSpec((1, nk, tt), lambda i, h: (h, 0, i))
    return pl.pallas_call(
        _router_kernel,
        out_shape=(f_shape,) * 4,
        grid=(m // tt, PEER_HEADS),
        in_specs=[pl.BlockSpec((tt, PEER_DQ), lambda i, h: (i, h)),
                  pl.BlockSpec((2, nk, PEER_DQ // 2), lambda i, h: (h, 0, 0))],
        out_specs=(s_spec,) * 4,
        compiler_params=_cparams(("arbitrary", "arbitrary")),
        name="peer_router",
    )(q, keys)


def _gelu(x):
    return 0.5 * x * (1.0 + lax.erf(x * (2.0 ** -0.5)))


def _peer_gate_kernel(h_ref, u_ref, c0_ref, e0_ref, r1_ref, e1_ref, o_ref, z_ref, *, te):
    e = pl.program_id(1)
    nsub = te // PEER_NKEYS
    tm = h_ref.shape[0]
    z_ref[...] = _dot_nt(u_ref[...], h_ref[...])
    for ii in range(nsub):
        i_idx = e * nsub + ii
        rows = slice(ii * PEER_NKEYS, (ii + 1) * PEER_NKEYS)
        c0_rows = [c0_ref[h, pl.ds(i_idx, 1), :] for h in range(PEER_HEADS)]
        e0_rows = [e0_ref[h, pl.ds(i_idx, 1), :] for h in range(PEER_HEADS)]
        for tc in range(tm // LANE):
            cols = slice(tc * LANE, (tc + 1) * LANE)
            w = None
            for h in range(PEER_HEADS):
                hit = r1_ref[h, :, cols] <= c0_rows[h][:, cols]
                term = jnp.where(hit, e1_ref[h, :, cols] * e0_rows[h][:, cols], 0.0)
                w = term if w is None else w + term
            p_t = _gelu(z_ref[rows, cols]) * w
            o_ref[cols, rows] = p_t.T.astype(o_ref.dtype)


def _peer_gate(h2, u_bf, c0t, e0t, r1t, e1t, tm, te):
    m, d = h2.shape
    ne = u_bf.shape[0]
    tm = min(tm, m)
    sspec = pl.BlockSpec((PEER_HEADS, PEER_NKEYS, tm), lambda i, e: (0, 0, i))
    return pl.pallas_call(
        functools.partial(_peer_gate_kernel, te=te),
        out_shape=jax.ShapeDtypeStruct((m, ne), BF16),
        grid=(m // tm, ne // te),
        in_specs=[pl.BlockSpec((tm, d), lambda i, e: (i, 0)),
                  pl.BlockSpec((te, d), lambda i, e: (e, 0)),
                  sspec, sspec, sspec, sspec],
        out_specs=pl.BlockSpec((tm, te), lambda i, e: (i, e)),
        scratch_shapes=[pltpu.VMEM((te, tm), F32)],
        compiler_params=_cparams(("arbitrary", "arbitrary")),
        name="peer_gate",
    )(h2, u_bf, c0t, e0t, r1t, e1t)


def _peer_out_kernel(p_ref, v_ref, x_ref, o_ref):
    @pl.when(pl.program_id(2) == 0)
    def _():
        o_ref[...] = x_ref[...]

    o_ref[...] += _dot(p_ref[...], v_ref[...])


def _peer_out(p, v_bf, x, tm, tn, tk):
    m, ne = p.shape
    d = v_bf.shape[1]
    tm, tn, tk = min(tm, m), min(tn, d), min(tk, ne)
    return pl.pallas_call(
        _peer_out_kernel,
        out_shape=jax.ShapeDtypeStruct((m, d), F32),
        grid=(m // tm, d // tn, ne // tk),
        in_specs=[pl.BlockSpec((tm, tk), lambda i, n, k: (i, k)),
                  pl.BlockSpec((tk, tn), lambda i, n, k: (k, n)),
                  pl.BlockSpec((tm, tn), lambda i, n, k: (i, n))],
        out_specs=pl.BlockSpec((tm, tn), lambda i, n, k: (i, n)),
        compiler_params=_cparams(("arbitrary", "arbitrary", "arbitrary")),
        name="peer_out",
    )(p, v_bf, x)


def kernel(x, mem, positions, ln1_g, w_in, att_q_g, att_k_g, rel_bias, mem_g, w_mem_kv, mem_q_g, mem_k_g,
           w_up_ret, w_up_att, w_up_mem, w_out, ln2_g, peer_w_q, peer_sub_keys, peer_u, peer_v):
    bsz, seq, d = x.shape
    m = bsz * seq
    depth = ln1_g.shape[0]
    xf = x.reshape(m, d)
    memf = mem.reshape(bsz * mem.shape[1], d)
    for l in range(depth):
        h = _rmsnorm(xf, ln1_g[l], 256)
        proj = _matmul_ring(h, jnp.swapaxes(w_in[l], 0, 1), tm=1024, tn=512, w_transposed=True, name="in_proj")
        y_ret = _retention(proj, positions, bsz, seq)
        kn, vt, ikb = _kprep(proj, att_k_g[l], 512)
        y_att = _sparse_attention(proj, kn, vt, ikb, att_q_g[l], rel_bias, bsz, seq)
        kv = _matmul(_rmsnorm(memf, mem_g[l], 256), w_mem_kv[l], tm=512, tn=512, name="mem_kv")
        y_mem = _memory_attention(proj, kv, mem_q_g[l], mem_k_g[l], bsz, seq, 512)
        merged = _merge(proj, y_ret, y_att, y_mem, w_up_ret[l], w_up_att[l], w_up_mem[l], d, 2048)
        xf = _matmul_ring(merged, w_out[l], tm=1024, tn=512, residual=xf, name="out_proj")
        h2 = _rmsnorm(xf, ln2_g[l], 256)
        q = _matmul(h2, peer_w_q[l], tm=1024, tn=512, name="peer_query")
        c0t, e0t, r1t, e1t = _peer_router(q, peer_sub_keys[l], 512)
        p = _peer_gate(h2, peer_u[l].astype(BF16), c0t, e0t, r1t, e1t, 512, 1024)
        xf = _peer_out(p, peer_v[l].astype(BF16), xf, 512, 2048, 2048)
    return xf.reshape(bsz, seq, d)
```

```python
import functools
import math

import jax
import jax.numpy as jnp
from jax import lax
from jax.experimental import pallas as pl
from jax.experimental.pallas import tpu as pltpu

F32 = jnp.float32
BF16 = jnp.bfloat16
I32 = jnp.int32

MEM_LEN = 256
RET_HEADS, RET_DK, RET_DV, RET_CHUNK = 8, 128, 256, 128
ATT_HEADS, ATT_DH = 8, 128
IDX_HEADS, IDX_DH = 16, 64
TOPK_MAX = 256
MEM_HEADS, MEM_DH = 4, 256
REL_BUCKETS, REL_MAX_DIST = 32, 128
PEER_HEADS, PEER_NKEYS, PEER_DQ, PEER_TOPK = 8, 128, 256, 16
EPS = 1e-6
ROPE_BASE = 10000.0

OFF_RQ = 0
OFF_RK = OFF_RQ + RET_HEADS * RET_DK
OFF_RV = OFF_RK + RET_HEADS * RET_DK
OFF_RG = OFF_RV + RET_HEADS * RET_DV
OFF_AQ = OFF_RG + RET_HEADS * RET_DV
OFF_AK = OFF_AQ + ATT_HEADS * ATT_DH
OFF_AV = OFF_AK + ATT_DH
OFF_IQ = OFF_AV + ATT_DH
OFF_IK = OFF_IQ + IDX_HEADS * IDX_DH
OFF_IW = OFF_IK + IDX_DH
OFF_MQ = OFF_IW + IDX_HEADS
OFF_GATE = OFF_MQ + MEM_HEADS * MEM_DH

LANE = 128
GATE_SHIFT = OFF_GATE % LANE
GATE_BASE = OFF_GATE - GATE_SHIFT
MQ_SHIFT = OFF_MQ - OFF_IK
VMEM_LIMIT = 56 * 1024 * 1024

INT_MIN = -(2 ** 31)
NEG_BIG = -1e30


def _cparams(sem):
    return pltpu.CompilerParams(dimension_semantics=sem, vmem_limit_bytes=VMEM_LIMIT)


def _dot_nt(a, b):
    return lax.dot_general(a, b, (((1,), (1,)), ((), ())), preferred_element_type=F32)


def _dot(a, b):
    return jnp.dot(a, b, preferred_element_type=F32)


def _rms(x):
    return x * lax.rsqrt(jnp.mean(x * x, axis=-1, keepdims=True) + EPS)


def _rmsnorm_kernel(x_ref, g_ref, o_ref):
    o_ref[...] = (_rms(x_ref[...].astype(F32)) * g_ref[...]).astype(o_ref.dtype)


def _rmsnorm(x, g, tm):
    m, d = x.shape
    tm = min(tm, m)
    return pl.pallas_call(
        _rmsnorm_kernel,
        out_shape=jax.ShapeDtypeStruct((m, d), BF16),
        grid=(m // tm,),
        in_specs=[pl.BlockSpec((tm, d), lambda i: (i, 0)),
                  pl.BlockSpec((1, d), lambda i: (0, 0))],
        out_specs=pl.BlockSpec((tm, d), lambda i: (i, 0)),
        compiler_params=_cparams(("arbitrary",)),
        name="rmsnorm",
    )(x, g.reshape(1, d))


def _mm_kernel(*refs, has_res, w_transposed):
    if has_res:
        a_ref, w_ref, r_ref, o_ref, wsc = refs
    else:
        a_ref, w_ref, o_ref, wsc = refs

    @pl.when(pl.program_id(1) == 0)
    def _():
        wsc[...] = w_ref[...].astype(BF16)

    acc = _dot_nt(a_ref[...], wsc[...]) if w_transposed else _dot(a_ref[...], wsc[...])
    if has_res:
        acc = acc + r_ref[...]
    o_ref[...] = acc.astype(o_ref.dtype)


def _matmul(a, w, *, tm, tn, residual=None, out_dtype=F32, w_transposed=False, name="matmul"):
    m, k = a.shape
    n = w.shape[0] if w_transposed else w.shape[1]
    tm = min(tm, m)
    tn = min(tn, n)
    if w_transposed:
        w_spec = pl.BlockSpec((tn, k), lambda j, i: (j, 0))
        w_scratch = pltpu.VMEM((tn, k), BF16)
    else:
        w_spec = pl.BlockSpec((k, tn), lambda j, i: (0, j))
        w_scratch = pltpu.VMEM((k, tn), BF16)
    in_specs = [pl.BlockSpec((tm, k), lambda j, i: (i, 0)), w_spec]
    args = [a, w]
    if residual is not None:
        in_specs.append(pl.BlockSpec((tm, tn), lambda j, i: (i, j)))
        args.append(residual)
    return pl.pallas_call(
        functools.partial(_mm_kernel, has_res=residual is not None, w_transposed=w_transposed),
        out_shape=jax.ShapeDtypeStruct((m, n), out_dtype),
        grid=(pl.cdiv(n, tn), m // tm),
        in_specs=in_specs,
        out_specs=pl.BlockSpec((tm, tn), lambda j, i: (i, j)),
        scratch_shapes=[w_scratch],
        compiler_params=_cparams(("arbitrary", "arbitrary")),
        name=name,
    )(*args)


A_RING = 3


def _mm_ring_kernel(*refs, has_res, w_transposed, tm, ni, nsteps):
    if has_res:
        a_hbm, w_ref, r_ref, o_ref, abuf, sem, wsc = refs
    else:
        a_hbm, w_ref, o_ref, abuf, sem, wsc = refs
    i = pl.program_id(1)
    s = pl.program_id(0) * ni + i

    def a_copy(step, slot):
        row = pl.multiple_of(lax.rem(step, ni) * tm, tm)
        return pltpu.make_async_copy(a_hbm.at[pl.ds(row, tm), :], abuf.at[slot], sem.at[slot])

    @pl.when(s == 0)
    def _():
        a_copy(0, 0).start()
        a_copy(1, 1).start()

    @pl.when(s + 2 < nsteps)
    def _():
        a_copy(s + 2, lax.rem(s + 2, A_RING)).start()

    @pl.when(i == 0)
    def _():
        wsc[...] = w_ref[...].astype(BF16)

    slot = lax.rem(s, A_RING)
    a_copy(s, slot).wait()
    acc = _dot_nt(abuf[slot], wsc[...]) if w_transposed else _dot(abuf[slot], wsc[...])
    if has_res:
        acc = acc + r_ref[...]
    o_ref[...] = acc.astype(o_ref.dtype)


def _matmul_ring(a, w, *, tm, tn, residual=None, out_dtype=F32, w_transposed=False, name="matmul_ring"):
    m, k = a.shape
    n = w.shape[0] if w_transposed else w.shape[1]
    tm = min(tm, m)
    tn = min(tn, n)
    ni = m // tm
    nj = pl.cdiv(n, tn)
    assert ni * nj >= 2
    w_block = (tn, k) if w_transposed else (k, tn)
    w_index = (lambda j, i: (j, 0)) if w_transposed else (lambda j, i: (0, j))
    in_specs = [pl.BlockSpec(memory_space=pl.ANY), pl.BlockSpec(w_block, w_index)]
    args = [a, w]
    if residual is not None:
        in_specs.append(pl.BlockSpec((tm, tn), lambda j, i: (i, j)))
        args.append(residual)
    return pl.pallas_call(
        functools.partial(_mm_ring_kernel, has_res=residual is not None, w_transposed=w_transposed,
                          tm=tm, ni=ni, nsteps=ni * nj),
        out_shape=jax.ShapeDtypeStruct((m, n), out_dtype),
        grid=(nj, ni),
        in_specs=in_specs,
        out_specs=pl.BlockSpec((tm, tn), lambda j, i: (i, j)),
        scratch_shapes=[pltpu.VMEM((A_RING, tm, k), BF16),
                        pltpu.SemaphoreType.DMA((A_RING,)),
                        pltpu.VMEM(w_block, BF16)],
        compiler_params=_cparams(("arbitrary", "arbitrary")),
        name=name,
    )(*args)


def _ret_kernel(cdec_ref, rq_ref, rk_ref, rv_ref, rg_ref, pos_ref, invf_ref, sign_ref,
                idec_ref, qdec_ref, kdec_ref, o_ref, st_ref):
    @pl.when(pl.program_id(1) == 0)
    def _():
        st_ref[...] = jnp.zeros_like(st_ref)

    ang = pos_ref[...].astype(F32) * invf_ref[...]
    cos = jnp.cos(ang)
    sin = jnp.sin(ang) * sign_ref[...]
    scale = RET_DK ** -0.5
    for h in range(RET_HEADS):
        q = rq_ref[:, h * RET_DK:(h + 1) * RET_DK]
        k = rk_ref[:, h * RET_DK:(h + 1) * RET_DK]
        q = q * cos + pltpu.roll(q, RET_DK // 2, 1) * sin
        k = (k * cos + pltpu.roll(k, RET_DK // 2, 1) * sin) * scale
        vb = rv_ref[:, h * RET_DV:(h + 1) * RET_DV].astype(BF16)
        s = _dot_nt(q.astype(BF16), k.astype(BF16)) * idec_ref[h]
        inner = _dot(s.astype(BF16), vb)
        st = st_ref[h]
        cross = _dot((q * qdec_ref[h]).astype(BF16), st.astype(BF16))
        kd = (k * kdec_ref[h]).astype(BF16)
        upd = lax.dot_general(kd, vb, (((0,), (0,)), ((), ())), preferred_element_type=F32)
        st_ref[h] = st * cdec_ref[h] + upd
        y = _rms(inner + cross)
        g = rg_ref[:, h * RET_DV:(h + 1) * RET_DV]
        o_ref[:, h * RET_DV:(h + 1) * RET_DV] = (g / (1.0 + jnp.exp(-g)) * y).astype(o_ref.dtype)


def _retention(proj, positions, bsz, seq):
    c = RET_CHUNK
    nc = seq // c
    m = bsz * seq
    hh = RET_HEADS
    log_gamma = jnp.log1p(-jnp.exp2(-5.0 - jnp.arange(hh, dtype=F32)))
    idx = jnp.arange(c, dtype=F32)
    diff = idx[:, None] - idx[None, :]
    idec = jnp.where(diff[None] >= 0,
                     jnp.exp(log_gamma[:, None, None] * jnp.maximum(diff, 0.0)[None]), 0.0)
    qdec = jnp.exp(log_gamma[:, None] * (idx + 1.0)[None])
    kdec = jnp.exp(log_gamma[:, None] * (c - 1.0 - idx)[None])
    cdec = jnp.exp(log_gamma * c)
    qdec_b = jnp.broadcast_to(qdec[:, :, None], (hh, c, RET_DK))
    kdec_b = jnp.broadcast_to(kdec[:, :, None], (hh, c, RET_DK))
    half = RET_DK // 2
    inv_freq = 1.0 / (ROPE_BASE ** (jnp.arange(half, dtype=F32) / half))
    invf = jnp.concatenate([inv_freq, inv_freq]).reshape(1, RET_DK)
    sign = jnp.concatenate([-jnp.ones((half,), F32), jnp.ones((half,), F32)]).reshape(1, RET_DK)
    qk_w = hh * RET_DK
    v_w = hh * RET_DV
    row = lambda b, i: b * nc + i
    const3 = lambda b, i: (0, 0, 0)
    return pl.pallas_call(
        _ret_kernel,
        out_shape=jax.ShapeDtypeStruct((m, v_w), BF16),
        grid=(bsz, nc),
        in_specs=[
            pl.BlockSpec(memory_space=pltpu.SMEM),
            pl.BlockSpec((c, qk_w), lambda b, i: (row(b, i), OFF_RQ // qk_w)),
            pl.BlockSpec((c, qk_w), lambda b, i: (row(b, i), OFF_RK // qk_w)),
            pl.BlockSpec((c, v_w), lambda b, i: (row(b, i), OFF_RV // v_w)),
            pl.BlockSpec((c, v_w), lambda b, i: (row(b, i), OFF_RG // v_w)),
            pl.BlockSpec((c, 1), lambda b, i: (row(b, i), 0)),
            pl.BlockSpec((1, RET_DK), lambda b, i: (0, 0)),
            pl.BlockSpec((1, RET_DK), lambda b, i: (0, 0)),
            pl.BlockSpec((hh, c, c), const3),
            pl.BlockSpec((hh, c, RET_DK), const3),
            pl.BlockSpec((hh, c, RET_DK), const3),
        ],
        out_specs=pl.BlockSpec((c, v_w), lambda b, i: (row(b, i), 0)),
        scratch_shapes=[pltpu.VMEM((hh, RET_DK, RET_DV), F32)],
        compiler_params=_cparams(("arbitrary", "arbitrary")),
        name="retention",
    )(cdec, proj, proj, proj, proj, positions.reshape(m, 1), invf, sign, idec, qdec_b, kdec_b)


def _kprep_kernel(ak_ref, av_ref, ikw_ref, kg_ref, kn_ref, vt_ref, ik_ref):
    kn_ref[...] = (_rms(ak_ref[...]) * kg_ref[...]).astype(BF16)
    vt_ref[...] = av_ref[...].T.astype(BF16)
    ik_ref[...] = ikw_ref[...].astype(BF16)


def _kprep(proj, att_k_g, tm):
    m = proj.shape[0]
    tm = min(tm, m)
    blk = lambda off: pl.BlockSpec((tm, LANE), lambda i: (i, off // LANE))
    nat = jax.ShapeDtypeStruct((m, LANE), BF16)
    nat_spec = pl.BlockSpec((tm, LANE), lambda i: (i, 0))
    return pl.pallas_call(
        _kprep_kernel,
        out_shape=(nat, jax.ShapeDtypeStruct((LANE, m), BF16), nat),
        grid=(m // tm,),
        in_specs=[blk(OFF_AK), blk(OFF_AV), blk(OFF_IK),
                  pl.BlockSpec((1, LANE), lambda i: (0, 0))],
        out_specs=(nat_spec, pl.BlockSpec((LANE, tm), lambda i: (0, i)), nat_spec),
        compiler_params=_cparams(("arbitrary",)),
        name="att_kprep",
    )(proj, proj, proj, att_k_g.reshape(1, ATT_DH))


IDX_TK = 512
FAR_TK = 512
M_INIT = -1e29


def _att_kernel(aq_ref, iq0_ref, iq1_ref, iq2_ref, iq3_ref, ikw_ref, kn_ref, vt_ref, ik_ref,
                qg_ref, bias_ref, o_ref, sc_ref, qt_ref, acc_ref, *, topk, tq, tki):
    qi = pl.program_id(1)
    t0 = qi * tq
    nkb = (t0 + tq + tki - 1) // tki
    col_t = t0 + lax.broadcasted_iota(I32, (1, tq), 1)
    sub = tki // LANE

    heads_t = []
    for ref in (iq0_ref, iq1_ref, iq2_ref, iq3_ref):
        for c in range(2):
            pair_t = ref[:, c * LANE:(c + 1) * LANE].T
            heads_t.append(pair_t[:IDX_DH])
            heads_t.append(pair_t[IDX_DH:])
    rhs_all = jnp.concatenate(heads_t, axis=1).astype(BF16)
    w_t = ikw_ref[...].T
    idx_scale = (IDX_DH ** -0.5) * (IDX_HEADS ** -0.5)

    def score_blk(kb, carry):
        k0 = pl.multiple_of(kb * tki, tki)
        ikb = ik_ref[pl.ds(k0, tki), :][:, :IDX_DH]
        lg = _dot(ikb, rhs_all)
        acc = None
        for h in range(IDX_HEADS):
            term = w_t[IDX_DH + h:IDX_DH + h + 1, :] * jnp.maximum(lg[:, h * tq:(h + 1) * tq], 0.0)
            acc = term if acc is None else acc + term
        acc = acc * idx_scale
        s_idx = k0 + lax.broadcasted_iota(I32, (tki, 1), 0)
        acc = jnp.where(s_idx <= col_t, acc, -jnp.inf)
        bits = pltpu.bitcast(acc, I32)
        sc_ref[pl.ds(k0, tki), :] = jnp.where(bits < 0, bits ^ 0x7FFFFFFF, bits)
        return carry

    lax.fori_loop(0, nkb, score_blk, 0)

    def count_ge(cand):
        def body(kb, cnt):
            k0 = pl.multiple_of(kb * tki, tki)
            c = jnp.where(sc_ref[pl.ds(k0, tki), :] >= cand, 1, 0)
            return cnt + jnp.sum(c.reshape(tki // 8, 8, tq), axis=0)
        cnt = lax.fori_loop(0, nkb, body, jnp.zeros((8, tq), I32))
        return jnp.sum(cnt, axis=0, keepdims=True)

    base = jnp.where(count_ge(jnp.zeros((1, tq), I32)) >= topk, 0, INT_MIN).astype(I32)

    def bit_body(i, base):
        cand = base | jnp.left_shift(jnp.int32(1), 30 - i)
        return jnp.where(count_ge(cand) >= topk, cand, base)

    tau = lax.fori_loop(0, 31, bit_body, base)

    qg = qg_ref[...]
    scale = ATT_DH ** -0.5
    tkf = min(FAR_TK, tki)
    fsub = tkf // LANE
    nfar = jnp.maximum(qi - 1, 0) // fsub

    for h in range(ATT_HEADS):
        qt_ref[h] = (_rms(aq_ref[:, h * ATT_DH:(h + 1) * ATT_DH]) * qg).T.astype(BF16)
    acc_ref[...] = jnp.zeros_like(acc_ref)

    def softmax_block(carry, k0, width, sel, bias_of):
        m_all, l_all = carry
        kblk = kn_ref[pl.ds(k0, width), :]
        vblk = vt_ref[:, pl.ds(k0, width)]
        m_rows, l_rows = [], []
        for h in range(ATT_HEADS):
            lg = jnp.where(sel, _dot(kblk, qt_ref[h]) * scale + bias_of(h), NEG_BIG)
            m_i = m_all[h:h + 1]
            m_new = jnp.maximum(m_i, jnp.max(lg, axis=0, keepdims=True))
            alpha = jnp.exp(m_i - m_new)
            p = jnp.exp(lg - m_new)
            m_rows.append(m_new)
            l_rows.append(alpha * l_all[h:h + 1] + jnp.sum(p, axis=0, keepdims=True))
            acc_ref[h] = alpha * acc_ref[h] + _dot(vblk, p.astype(BF16))
        return jnp.concatenate(m_rows, axis=0), jnp.concatenate(l_rows, axis=0)

    def far_blk(kb, carry):
        k0 = pl.multiple_of(kb * tkf, tkf)
        sel = sc_ref[pl.ds(k0, tkf), :] >= tau
        return softmax_block(carry, k0, tkf, sel,
                             lambda h: jnp.concatenate([bias_ref[2, h]] * fsub, axis=0))

    def near_blk(j, carry):
        k0 = pl.multiple_of(j * LANE, LANE)
        s_idx = k0 + lax.broadcasted_iota(I32, (LANE, 1), 0)
        sel = jnp.logical_and(sc_ref[pl.ds(k0, LANE), :] >= tau, s_idx <= col_t)
        d = jnp.minimum(qi - j, 2)
        return softmax_block(carry, k0, LANE, sel, lambda h: bias_ref[d, h])

    init = (jnp.full((ATT_HEADS, tq), M_INIT, F32), jnp.zeros((ATT_HEADS, tq), F32))
    carry = lax.fori_loop(0, nfar, far_blk, init)
    _, l_f = lax.fori_loop(nfar * fsub, qi + 1, near_blk, carry)
    for h in range(ATT_HEADS):
        o_ref[:, h * ATT_DH:(h + 1) * ATT_DH] = (acc_ref[h] / l_f[h:h + 1]).T.astype(o_ref.dtype)


def _t5_bucket(dist):
    n = jnp.maximum(dist, 0)
    max_exact = REL_BUCKETS // 2
    large = max_exact + (jnp.log(jnp.maximum(n, 1).astype(F32) / max_exact)
                         / math.log(REL_MAX_DIST / max_exact) * (REL_BUCKETS - max_exact)).astype(I32)
    large = jnp.minimum(large, REL_BUCKETS - 1)
    return jnp.where(n < max_exact, n, large)


def _sparse_attention(proj, kn, vt, ikb, att_q_g, rel_bias, bsz, seq):
    tq = LANE
    m = bsz * seq
    nq = seq // tq
    topk = min(TOPK_MAX, seq // 4)
    tki = min(IDX_TK, seq)
    kk = jnp.arange(LANE, dtype=I32)[:, None]
    qq = jnp.arange(tq, dtype=I32)[None, :]
    dist = jnp.stack([d * LANE + qq - kk for d in range(3)])
    onehot = (_t5_bucket(dist)[..., None] == jnp.arange(REL_BUCKETS, dtype=I32)).astype(F32)
    bias = jnp.einsum("dkqb,bh->dhkq", onehot, rel_bias.astype(F32), precision=lax.Precision.HIGHEST)
    qw = ATT_HEADS * ATT_DH
    row = lambda b, i: b * nq + i
    iq_spec = lambda p: pl.BlockSpec((tq, 2 * LANE), lambda b, i: (row(b, i), OFF_IQ // (2 * LANE) + p))
    kspec = pl.BlockSpec((seq, LANE), lambda b, i: (b, 0))
    return pl.pallas_call(
        functools.partial(_att_kernel, topk=topk, tq=tq, tki=tki),
        out_shape=jax.ShapeDtypeStruct((m, qw), BF16),
        grid=(bsz, nq),
        in_specs=[
            pl.BlockSpec((tq, qw), lambda b, i: (row(b, i), OFF_AQ // qw)),
            iq_spec(0), iq_spec(1), iq_spec(2), iq_spec(3),
            pl.BlockSpec((tq, LANE), lambda b, i: (row(b, i), OFF_IK // LANE)),
            kspec,
            pl.BlockSpec((LANE, seq), lambda b, i: (0, b)),
            kspec,
            pl.BlockSpec((1, ATT_DH), lambda b, i: (0, 0)),
            pl.BlockSpec((3, ATT_HEADS, LANE, tq), lambda b, i: (0, 0, 0, 0)),
        ],
        out_specs=pl.BlockSpec((tq, qw), lambda b, i: (row(b, i), 0)),
        scratch_shapes=[pltpu.VMEM((seq, tq), I32),
                        pltpu.VMEM((ATT_HEADS, ATT_DH, tq), BF16),
                        pltpu.VMEM((ATT_HEADS, ATT_DH, tq), F32)],
        compiler_params=_cparams(("arbitrary", "arbitrary")),
        name="sparse_attention",
    )(proj, proj, proj, proj, proj, proj, kn, vt, ikb, att_q_g.reshape(1, ATT_DH), bias)


MQ_WIN = 3 * LANE


def _mem_kernel(w0_ref, w1_ref, w2_ref, kv_ref, qg_ref, kg_ref, o_ref):
    win = jnp.concatenate([w0_ref[...], w1_ref[...], w2_ref[...]], axis=1)
    mq = win[:, MQ_SHIFT:MQ_SHIFT + MEM_HEADS * MEM_DH]
    qg = qg_ref[...]
    kg = kg_ref[...]
    scale = MEM_DH ** -0.5
    kvw = MEM_HEADS * MEM_DH
    for h in range(MEM_HEADS):
        q = (_rms(mq[:, h * MEM_DH:(h + 1) * MEM_DH]) * qg).astype(BF16)
        k = (_rms(kv_ref[:, h * MEM_DH:(h + 1) * MEM_DH]) * kg).astype(BF16)
        v = kv_ref[:, kvw + h * MEM_DH:kvw + (h + 1) * MEM_DH].astype(BF16)
        lg = _dot_nt(q, k) * scale
        p = jnp.exp(lg - jnp.max(lg, axis=-1, keepdims=True))
        p = p / jnp.sum(p, axis=-1, keepdims=True)
        o_ref[:, h * MEM_DH:(h + 1) * MEM_DH] = _dot(p.astype(BF16), v).astype(o_ref.dtype)


def _memory_attention(proj, kv, mem_q_g, mem_k_g, bsz, seq, ts):
    ts = min(ts, seq)
    ns = seq // ts
    m = bsz * seq
    qw = MEM_HEADS * MEM_DH
    row = lambda b, i: b * ns + i
    wspec = lambda p: pl.BlockSpec((ts, MQ_WIN), lambda b, i: (row(b, i), OFF_IK // MQ_WIN + p))
    return pl.pallas_call(
        _mem_kernel,
        out_shape=jax.ShapeDtypeStruct((m, qw), BF16),
        grid=(bsz, ns),
        in_specs=[wspec(0), wspec(1), wspec(2),
                  pl.BlockSpec((MEM_LEN, 2 * qw), lambda b, i: (b, 0)),
                  pl.BlockSpec((1, MEM_DH), lambda b, i: (0, 0)),
                  pl.BlockSpec((1, MEM_DH), lambda b, i: (0, 0))],
        out_specs=pl.BlockSpec((ts, qw), lambda b, i: (row(b, i), 0)),
        compiler_params=_cparams(("arbitrary", "arbitrary")),
        name="memory_attention",
    )(proj, proj, proj, kv, mem_q_g.reshape(1, MEM_DH), mem_k_g.reshape(1, MEM_DH))


MERGE_TN = 256


def _merge_kernel(yr_ref, ya_ref, ym_ref, wr_ref, wa_ref, wm_ref,
                  g0_ref, e0_ref, g1_ref, e1_ref, g2_ref, e2_ref, o_ref):
    acc = None
    for y_ref, w_ref, g_ref, e_ref in ((yr_ref, wr_ref, g0_ref, e0_ref),
                                       (ya_ref, wa_ref, g1_ref, e1_ref),
                                       (ym_ref, wm_ref, g2_ref, e2_ref)):
        g = jnp.concatenate([g_ref[...], e_ref[...]], axis=1)[:, GATE_SHIFT:GATE_SHIFT + MERGE_TN]
        y = _dot(y_ref[...], w_ref[...].astype(BF16))
        term = y / (1.0 + jnp.exp(-g))
        acc = term if acc is None else acc + term
    o_ref[...] = acc.astype(o_ref.dtype)


def _merge(proj, y_ret, y_att, y_mem, w_up_ret, w_up_att, w_up_mem, d_model, tm):
    m = y_ret.shape[0]
    tm = min(tm, m)
    tn = MERGE_TN
    yspec = lambda y: pl.BlockSpec((tm, y.shape[1]), lambda i, n: (i, 0), pipeline_mode=pl.Buffered(1))
    wspec = lambda w: pl.BlockSpec((w.shape[0], tn), lambda i, n: (0, n))
    gspecs = []
    for br in range(3):
        base = GATE_BASE + br * d_model
        gspecs.append(pl.BlockSpec((tm, tn), lambda i, n, base=base: (i, base // tn + n)))
        gspecs.append(pl.BlockSpec((tm, LANE), lambda i, n, base=base: (i, base // LANE + (n + 1) * (tn // LANE))))
    return pl.pallas_call(
        _merge_kernel,
        out_shape=jax.ShapeDtypeStruct((m, d_model), BF16),
        grid=(m // tm, d_model // tn),
        in_specs=[yspec(y_ret), yspec(y_att), yspec(y_mem),
                  wspec(w_up_ret), wspec(w_up_att), wspec(w_up_mem)] + gspecs,
        out_specs=pl.BlockSpec((tm, tn), lambda i, n: (i, n)),
        compiler_params=_cparams(("arbitrary", "arbitrary")),
        name="gated_merge",
    )(y_ret, y_att, y_mem, w_up_ret, w_up_att, w_up_mem, *([proj] * 6))


RANK_NONE = 64.0


def _top_rows(vals, k, with_rank=False):
    rows = []
    rank = jnp.full(vals.shape, RANK_NONE, F32) if with_rank else None
    for r in range(k):
        mx = jnp.max(vals, axis=0, keepdims=True)
        rows.append(mx)
        top = vals == mx
        if with_rank:
            rank = jnp.where(top, r + 1.0, rank)
        vals = jnp.where(top, -jnp.inf, vals)
    return (rows, rank) if with_rank else rows


def _router_kernel(q_ref, keys_ref, u_ref, v_ref, c0_ref, e0_ref, r1_ref, e1_ref, ub_ref, vb_ref):
    ub_ref[...] = u_ref[...].astype(ub_ref.dtype)
    vb_ref[...] = v_ref[...].astype(vb_ref.dtype)
    kk = PEER_TOPK
    s_t = []
    for c in range(2):
        qs = q_ref[:, c * LANE:(c + 1) * LANE].astype(BF16)
        s_t.append(_dot_nt(keys_ref[c].astype(BF16), qs))
    a = _top_rows(s_t[0], kk)
    b, rank1 = _top_rows(s_t[1], kk, with_rank=True)
    amat = jnp.concatenate(a, axis=0)
    bmat = jnp.concatenate(b, axis=0)
    half = kk // 2
    cand = jnp.concatenate([a[0] + bmat] + [a[r] + bmat[:half] for r in range(1, half)]
                           + [amat[half:] + b[0]], axis=0)
    tv = _top_rows(cand, kk)
    tau = tv[kk - 1]
    z = None
    for r in range(kk):
        e = jnp.exp(tv[r] - tv[0])
        z = e if z is None else z + e
    cnt = jnp.zeros(amat.shape, F32)
    for j in range(kk):
        cnt = cnt + jnp.where(amat + b[j] >= tau, 1.0, 0.0)
    count0 = jnp.zeros(s_t[0].shape, F32)
    for r in range(kk):
        count0 = jnp.where(s_t[0] == a[r], cnt[r:r + 1], count0)
    c0_ref[0] = count0
    e0_ref[0] = jnp.exp(s_t[0] - a[0]) * (1.0 / z)
    r1_ref[0] = rank1
    e1_ref[0] = jnp.exp(s_t[1] - b[0])


def _peer_router(q, sub_keys, u_tab, v_tab, tt):
    m = q.shape[0]
    tt = min(tt, m)
    nk = PEER_NKEYS
    ne, d = u_tab.shape
    steps = (m // tt) * PEER_HEADS
    slab = ne // steps
    assert slab * steps == ne and slab % 16 == 0
    keys = sub_keys.reshape(PEER_HEADS * 2, nk, PEER_DQ // 2)
    f_shape = jax.ShapeDtypeStruct((PEER_HEADS, nk, m), F32)
    t_shape = jax.ShapeDtypeStruct((ne, d), BF16)
    s_spec = pl.BlockSpec((1, nk, tt), lambda i, h: (h, 0, i))
    t_spec = pl.BlockSpec((slab, d), lambda i, h: (i * PEER_HEADS + h, 0))
    return pl.pallas_call(
        _router_kernel,
        out_shape=(f_shape,) * 4 + (t_shape,) * 2,
        grid=(m // tt, PEER_HEADS),
        in_specs=[pl.BlockSpec((tt, PEER_DQ), lambda i, h: (i, h)),
                  pl.BlockSpec((2, nk, PEER_DQ // 2), lambda i, h: (h, 0, 0)),
                  t_spec, t_spec],
        out_specs=(s_spec,) * 4 + (t_spec,) * 2,
        compiler_params=_cparams(("arbitrary", "arbitrary")),
        name="peer_router",
    )(q, keys, u_tab, v_tab)


def _gelu(x):
    return 0.5 * x * (1.0 + lax.erf(x * (2.0 ** -0.5)))


def _peer_gate_kernel(h_ref, u_ref, c0_ref, e0_ref, r1_ref, e1_ref, o_ref, z_ref, *, te):
    e = pl.program_id(1)
    nsub = te // PEER_NKEYS
    tm = h_ref.shape[0]
    z_ref[...] = _dot_nt(u_ref[...], h_ref[...])
    for ii in range(nsub):
        i_idx = e * nsub + ii
        rows = slice(ii * PEER_NKEYS, (ii + 1) * PEER_NKEYS)
        c0_rows = [c0_ref[h, pl.ds(i_idx, 1), :] for h in range(PEER_HEADS)]
        e0_rows = [e0_ref[h, pl.ds(i_idx, 1), :] for h in range(PEER_HEADS)]
        for tc in range(tm // LANE):
            cols = slice(tc * LANE, (tc + 1) * LANE)
            w = None
            for h in range(PEER_HEADS):
                hit = r1_ref[h, :, cols] <= c0_rows[h][:, cols]
                term = jnp.where(hit, e1_ref[h, :, cols] * e0_rows[h][:, cols], 0.0)
                w = term if w is None else w + term
            p_t = _gelu(z_ref[rows, cols]) * w
            o_ref[cols, rows] = p_t.T.astype(o_ref.dtype)


def _peer_gate(h2, u_bf, c0t, e0t, r1t, e1t, tm, te):
    m, d = h2.shape
    ne = u_bf.shape[0]
    tm = min(tm, m)
    sspec = pl.BlockSpec((PEER_HEADS, PEER_NKEYS, tm), lambda i, e: (0, 0, i))
    return pl.pallas_call(
        functools.partial(_peer_gate_kernel, te=te),
        out_shape=jax.ShapeDtypeStruct((m, ne), BF16),
        grid=(m // tm, ne // te),
        in_specs=[pl.BlockSpec((tm, d), lambda i, e: (i, 0)),
                  pl.BlockSpec((te, d), lambda i, e: (e, 0)),
                  sspec, sspec, sspec, sspec],
        out_specs=pl.BlockSpec((tm, te), lambda i, e: (i, e)),
        scratch_shapes=[pltpu.VMEM((te, tm), F32)],
        compiler_params=_cparams(("arbitrary", "arbitrary")),
        name="peer_gate",
    )(h2, u_bf, c0t, e0t, r1t, e1t)


def _peer_out_kernel(p_ref, v_ref, x_ref, o_ref):
    @pl.when(pl.program_id(2) == 0)
    def _():
        o_ref[...] = x_ref[...]

    o_ref[...] += _dot(p_ref[...], v_ref[...])


def _peer_out(p, v_bf, x, tm, tn, tk):
    m, ne = p.shape
    d = v_bf.shape[1]
    tm, tn, tk = min(tm, m), min(tn, d), min(tk, ne)
    return pl.pallas_call(
        _peer_out_kernel,
        out_shape=jax.ShapeDtypeStruct((m, d), F32),
        grid=(m // tm, d // tn, ne // tk),
        in_specs=[pl.BlockSpec((tm, tk), lambda i, n, k: (i, k)),
                  pl.BlockSpec((tk, tn), lambda i, n, k: (k, n)),
                  pl.BlockSpec((tm, tn), lambda i, n, k: (i, n))],
        out_specs=pl.BlockSpec((tm, tn), lambda i, n, k: (i, n)),
        compiler_params=_cparams(("arbitrary", "arbitrary", "arbitrary")),
        name="peer_out",
    )(p, v_bf, x)


def kernel(x, mem, positions, ln1_g, w_in, att_q_g, att_k_g, rel_bias, mem_g, w_mem_kv, mem_q_g, mem_k_g,
           w_up_ret, w_up_att, w_up_mem, w_out, ln2_g, peer_w_q, peer_sub_keys, peer_u, peer_v):
    bsz, seq, d = x.shape
    m = bsz * seq
    depth = ln1_g.shape[0]
    xf = x.reshape(m, d)
    memf = mem.reshape(bsz * mem.shape[1], d)
    for l in range(depth):
        h = _rmsnorm(xf, ln1_g[l], 256)
        proj = _matmul_ring(h, jnp.swapaxes(w_in[l], 0, 1), tm=1024, tn=512, w_transposed=True, name="in_proj")
        y_ret = _retention(proj, positions, bsz, seq)
        kn, vt, ikb = _kprep(proj, att_k_g[l], 512)
        y_att = _sparse_attention(proj, kn, vt, ikb, att_q_g[l], rel_bias, bsz, seq)
        kv = _matmul(_rmsnorm(memf, mem_g[l], 256), w_mem_kv[l], tm=512, tn=512, name="mem_kv")
        y_mem = _memory_attention(proj, kv, mem_q_g[l], mem_k_g[l], bsz, seq, 512)
        merged = _merge(proj, y_ret, y_att, y_mem, w_up_ret[l], w_up_att[l], w_up_mem[l], d, 2048)
        xf = _matmul_ring(merged, w_out[l], tm=1024, tn=512, residual=xf, name="out_proj")
        h2 = _rmsnorm(xf, ln2_g[l], 256)
        q = _matmul(h2, peer_w_q[l], tm=1024, tn=512, name="peer_query")
        c0t, e0t, r1t, e1t, u_bf, v_bf = _peer_router(q, peer_sub_keys[l], peer_u[l], peer_v[l], 512)
        p = _peer_gate(h2, u_bf, c0t, e0t, r1t, e1t, 512, 1024)
        xf = _peer_out(p, v_bf, xf, 512, 2048, 2048)
    return xf.reshape(bsz, seq, d)
```

```python
import functools
import math

import jax
import jax.numpy as jnp
from jax import lax
from jax.experimental import pallas as pl
from jax.experimental.pallas import tpu as pltpu

F32 = jnp.float32
BF16 = jnp.bfloat16
I32 = jnp.int32

MEM_LEN = 256
RET_HEADS, RET_DK, RET_DV, RET_CHUNK = 8, 128, 256, 128
ATT_HEADS, ATT_DH = 8, 128
IDX_HEADS, IDX_DH = 16, 64
TOPK_MAX = 256
MEM_HEADS, MEM_DH = 4, 256
REL_BUCKETS, REL_MAX_DIST = 32, 128
PEER_HEADS, PEER_NKEYS, PEER_DQ, PEER_TOPK = 8, 128, 256, 16
EPS = 1e-6
ROPE_BASE = 10000.0

OFF_RQ = 0
OFF_RK = OFF_RQ + RET_HEADS * RET_DK
OFF_RV = OFF_RK + RET_HEADS * RET_DK
OFF_RG = OFF_RV + RET_HEADS * RET_DV
OFF_AQ = OFF_RG + RET_HEADS * RET_DV
OFF_AK = OFF_AQ + ATT_HEADS * ATT_DH
OFF_AV = OFF_AK + ATT_DH
OFF_IQ = OFF_AV + ATT_DH
OFF_IK = OFF_IQ + IDX_HEADS * IDX_DH
OFF_IW = OFF_IK + IDX_DH
OFF_MQ = OFF_IW + IDX_HEADS
OFF_GATE = OFF_MQ + MEM_HEADS * MEM_DH

LANE = 128
GATE_SHIFT = OFF_GATE % LANE
GATE_BASE = OFF_GATE - GATE_SHIFT
MQ_SHIFT = OFF_MQ - OFF_IK
VMEM_LIMIT = 56 * 1024 * 1024


class Tiles:
    NORM_ROWS = 256
    IN_PROJ = (1024, 512)
    OUT_PROJ = (1024, 512)
    PEER_QUERY = (1024, 512)
    MEM_KV = (512, 512)
    KPREP_ROWS = 512
    MEM_ATT_ROWS = 512
    MERGE_ROWS = 2048
    ROUTER_TOKENS = 512
    GATE = (512, 1024)
    PEER_OUT = (512, 2048, 2048)

INT_MIN = -(2 ** 31)
NEG_BIG = -1e30


def _cparams(sem):
    return pltpu.CompilerParams(dimension_semantics=sem, vmem_limit_bytes=VMEM_LIMIT)


def _dot_nt(a, b):
    return lax.dot_general(a, b, (((1,), (1,)), ((), ())), preferred_element_type=F32)


def _dot(a, b):
    return jnp.dot(a, b, preferred_element_type=F32)


def _rms(x):
    return x * lax.rsqrt(jnp.mean(x * x, axis=-1, keepdims=True) + EPS)


def _rmsnorm_kernel(x_ref, g_ref, o_ref):
    o_ref[...] = (_rms(x_ref[...].astype(F32)) * g_ref[...]).astype(o_ref.dtype)


def _rmsnorm(x, g, tm):
    m, d = x.shape
    tm = min(tm, m)
    return pl.pallas_call(
        _rmsnorm_kernel,
        out_shape=jax.ShapeDtypeStruct((m, d), BF16),
        grid=(m // tm,),
        in_specs=[pl.BlockSpec((tm, d), lambda i: (i, 0)),
                  pl.BlockSpec((1, d), lambda i: (0, 0))],
        out_specs=pl.BlockSpec((tm, d), lambda i: (i, 0)),
        compiler_params=_cparams(("arbitrary",)),
        name="rmsnorm",
    )(x, g.reshape(1, d))


def _mm_kernel(*refs, has_res, w_transposed):
    if has_res:
        a_ref, w_ref, r_ref, o_ref, wsc = refs
    else:
        a_ref, w_ref, o_ref, wsc = refs

    @pl.when(pl.program_id(1) == 0)
    def _():
        wsc[...] = w_ref[...].astype(BF16)

    acc = _dot_nt(a_ref[...], wsc[...]) if w_transposed else _dot(a_ref[...], wsc[...])
    if has_res:
        acc = acc + r_ref[...]
    o_ref[...] = acc.astype(o_ref.dtype)


def _matmul(a, w, *, tm, tn, residual=None, out_dtype=F32, w_transposed=False, name="matmul"):
    m, k = a.shape
    n = w.shape[0] if w_transposed else w.shape[1]
    tm = min(tm, m)
    tn = min(tn, n)
    if w_transposed:
        w_spec = pl.BlockSpec((tn, k), lambda j, i: (j, 0))
        w_scratch = pltpu.VMEM((tn, k), BF16)
    else:
        w_spec = pl.BlockSpec((k, tn), lambda j, i: (0, j))
        w_scratch = pltpu.VMEM((k, tn), BF16)
    in_specs = [pl.BlockSpec((tm, k), lambda j, i: (i, 0)), w_spec]
    args = [a, w]
    if residual is not None:
        in_specs.append(pl.BlockSpec((tm, tn), lambda j, i: (i, j)))
        args.append(residual)
    return pl.pallas_call(
        functools.partial(_mm_kernel, has_res=residual is not None, w_transposed=w_transposed),
        out_shape=jax.ShapeDtypeStruct((m, n), out_dtype),
        grid=(pl.cdiv(n, tn), m // tm),
        in_specs=in_specs,
        out_specs=pl.BlockSpec((tm, tn), lambda j, i: (i, j)),
        scratch_shapes=[w_scratch],
        compiler_params=_cparams(("arbitrary", "arbitrary")),
        name=name,
    )(*args)


A_RING = 3


def _mm_ring_kernel(*refs, has_res, w_transposed, tm, ni, nsteps):
    if has_res:
        a_hbm, w_ref, r_ref, o_ref, abuf, sem, wsc = refs
    else:
        a_hbm, w_ref, o_ref, abuf, sem, wsc = refs
    i = pl.program_id(1)
    s = pl.program_id(0) * ni + i

    def a_copy(step, slot):
        row = pl.multiple_of(lax.rem(step, ni) * tm, tm)
        return pltpu.make_async_copy(a_hbm.at[pl.ds(row, tm), :], abuf.at[slot], sem.at[slot])

    @pl.when(s == 0)
    def _():
        a_copy(0, 0).start()
        a_copy(1, 1).start()

    @pl.when(s + 2 < nsteps)
    def _():
        a_copy(s + 2, lax.rem(s + 2, A_RING)).start()

    @pl.when(i == 0)
    def _():
        wsc[...] = w_ref[...].astype(BF16)

    slot = lax.rem(s, A_RING)
    a_copy(s, slot).wait()
    acc = _dot_nt(abuf[slot], wsc[...]) if w_transposed else _dot(abuf[slot], wsc[...])
    if has_res:
        acc = acc + r_ref[...]
    o_ref[...] = acc.astype(o_ref.dtype)


def _matmul_ring(a, w, *, tm, tn, residual=None, out_dtype=F32, w_transposed=False, name="matmul_ring"):
    m, k = a.shape
    n = w.shape[0] if w_transposed else w.shape[1]
    tm = min(tm, m)
    tn = min(tn, n)
    ni = m // tm
    nj = pl.cdiv(n, tn)
    assert ni * nj >= 2
    w_block = (tn, k) if w_transposed else (k, tn)
    w_index = (lambda j, i: (j, 0)) if w_transposed else (lambda j, i: (0, j))
    in_specs = [pl.BlockSpec(memory_space=pl.ANY), pl.BlockSpec(w_block, w_index)]
    args = [a, w]
    if residual is not None:
        in_specs.append(pl.BlockSpec((tm, tn), lambda j, i: (i, j)))
        args.append(residual)
    return pl.pallas_call(
        functools.partial(_mm_ring_kernel, has_res=residual is not None, w_transposed=w_transposed,
                          tm=tm, ni=ni, nsteps=ni * nj),
        out_shape=jax.ShapeDtypeStruct((m, n), out_dtype),
        grid=(nj, ni),
        in_specs=in_specs,
        out_specs=pl.BlockSpec((tm, tn), lambda j, i: (i, j)),
        scratch_shapes=[pltpu.VMEM((A_RING, tm, k), BF16),
                        pltpu.SemaphoreType.DMA((A_RING,)),
                        pltpu.VMEM(w_block, BF16)],
        compiler_params=_cparams(("arbitrary", "arbitrary")),
        name=name,
    )(*args)


def _ret_kernel(cdec_ref, rq_ref, rk_ref, rv_ref, rg_ref, pos_ref, invf_ref, sign_ref,
                idec_ref, qdec_ref, kdec_ref, o_ref, st_ref):
    @pl.when(pl.program_id(1) == 0)
    def _():
        st_ref[...] = jnp.zeros_like(st_ref)

    ang = pos_ref[...].astype(F32) * invf_ref[...]
    cos = jnp.cos(ang)
    sin = jnp.sin(ang) * sign_ref[...]
    scale = RET_DK ** -0.5
    for h in range(RET_HEADS):
        q = rq_ref[:, h * RET_DK:(h + 1) * RET_DK]
        k = rk_ref[:, h * RET_DK:(h + 1) * RET_DK]
        q = q * cos + pltpu.roll(q, RET_DK // 2, 1) * sin
        k = (k * cos + pltpu.roll(k, RET_DK // 2, 1) * sin) * scale
        vb = rv_ref[:, h * RET_DV:(h + 1) * RET_DV].astype(BF16)
        s = _dot_nt(q.astype(BF16), k.astype(BF16)) * idec_ref[h]
        inner = _dot(s.astype(BF16), vb)
        st = st_ref[h]
        cross = _dot((q * qdec_ref[h]).astype(BF16), st.astype(BF16))
        kd = (k * kdec_ref[h]).astype(BF16)
        upd = lax.dot_general(kd, vb, (((0,), (0,)), ((), ())), preferred_element_type=F32)
        st_ref[h] = st * cdec_ref[h] + upd
        y = _rms(inner + cross)
        g = rg_ref[:, h * RET_DV:(h + 1) * RET_DV]
        o_ref[:, h * RET_DV:(h + 1) * RET_DV] = (g / (1.0 + jnp.exp(-g)) * y).astype(o_ref.dtype)


def _retention(proj, positions, bsz, seq):
    c = RET_CHUNK
    nc = seq // c
    m = bsz * seq
    hh = RET_HEADS
    log_gamma = jnp.log1p(-jnp.exp2(-5.0 - jnp.arange(hh, dtype=F32)))
    idx = jnp.arange(c, dtype=F32)
    diff = idx[:, None] - idx[None, :]
    idec = jnp.where(diff[None] >= 0,
                     jnp.exp(log_gamma[:, None, None] * jnp.maximum(diff, 0.0)[None]), 0.0)
    qdec = jnp.exp(log_gamma[:, None] * (idx + 1.0)[None])
    kdec = jnp.exp(log_gamma[:, None] * (c - 1.0 - idx)[None])
    cdec = jnp.exp(log_gamma * c)
    qdec_b = jnp.broadcast_to(qdec[:, :, None], (hh, c, RET_DK))
    kdec_b = jnp.broadcast_to(kdec[:, :, None], (hh, c, RET_DK))
    half = RET_DK // 2
    inv_freq = 1.0 / (ROPE_BASE ** (jnp.arange(half, dtype=F32) / half))
    invf = jnp.concatenate([inv_freq, inv_freq]).reshape(1, RET_DK)
    sign = jnp.concatenate([-jnp.ones((half,), F32), jnp.ones((half,), F32)]).reshape(1, RET_DK)
    qk_w = hh * RET_DK
    v_w = hh * RET_DV
    row = lambda b, i: b * nc + i
    const3 = lambda b, i: (0, 0, 0)
    return pl.pallas_call(
        _ret_kernel,
        out_shape=jax.ShapeDtypeStruct((m, v_w), BF16),
        grid=(bsz, nc),
        in_specs=[
            pl.BlockSpec(memory_space=pltpu.SMEM),
            pl.BlockSpec((c, qk_w), lambda b, i: (row(b, i), OFF_RQ // qk_w)),
            pl.BlockSpec((c, qk_w), lambda b, i: (row(b, i), OFF_RK // qk_w)),
            pl.BlockSpec((c, v_w), lambda b, i: (row(b, i), OFF_RV // v_w)),
            pl.BlockSpec((c, v_w), lambda b, i: (row(b, i), OFF_RG // v_w)),
            pl.BlockSpec((c, 1), lambda b, i: (row(b, i), 0)),
            pl.BlockSpec((1, RET_DK), lambda b, i: (0, 0)),
            pl.BlockSpec((1, RET_DK), lambda b, i: (0, 0)),
            pl.BlockSpec((hh, c, c), const3),
            pl.BlockSpec((hh, c, RET_DK), const3),
            pl.BlockSpec((hh, c, RET_DK), const3),
        ],
        out_specs=pl.BlockSpec((c, v_w), lambda b, i: (row(b, i), 0)),
        scratch_shapes=[pltpu.VMEM((hh, RET_DK, RET_DV), F32)],
        compiler_params=_cparams(("arbitrary", "arbitrary")),
        name="retention",
    )(cdec, proj, proj, proj, proj, positions.reshape(m, 1), invf, sign, idec, qdec_b, kdec_b)


def _kprep_kernel(ak_ref, av_ref, ikw_ref, kg_ref, kn_ref, vt_ref, ik_ref):
    kn_ref[...] = (_rms(ak_ref[...]) * kg_ref[...]).astype(BF16)
    vt_ref[...] = av_ref[...].T.astype(BF16)
    ik_ref[...] = ikw_ref[...].astype(BF16)


def _kprep(proj, att_k_g, tm):
    m = proj.shape[0]
    tm = min(tm, m)
    blk = lambda off: pl.BlockSpec((tm, LANE), lambda i: (i, off // LANE))
    nat = jax.ShapeDtypeStruct((m, LANE), BF16)
    nat_spec = pl.BlockSpec((tm, LANE), lambda i: (i, 0))
    return pl.pallas_call(
        _kprep_kernel,
        out_shape=(nat, jax.ShapeDtypeStruct((LANE, m), BF16), nat),
        grid=(m // tm,),
        in_specs=[blk(OFF_AK), blk(OFF_AV), blk(OFF_IK),
                  pl.BlockSpec((1, LANE), lambda i: (0, 0))],
        out_specs=(nat_spec, pl.BlockSpec((LANE, tm), lambda i: (0, i)), nat_spec),
        compiler_params=_cparams(("arbitrary",)),
        name="att_kprep",
    )(proj, proj, proj, att_k_g.reshape(1, ATT_DH))


IDX_TK = 512
FAR_TK = 512
M_INIT = -1e29


def _att_kernel(aq_ref, iq0_ref, iq1_ref, iq2_ref, iq3_ref, ikw_ref, kn_ref, vt_ref, ik_ref,
                qg_ref, bias_ref, o_ref, sc_ref, qt_ref, acc_ref, *, topk, tq, tki):
    qi = pl.program_id(1)
    t0 = qi * tq
    nkb = (t0 + tq + tki - 1) // tki
    col_t = t0 + lax.broadcasted_iota(I32, (1, tq), 1)
    sub = tki // LANE

    heads_t = []
    for ref in (iq0_ref, iq1_ref, iq2_ref, iq3_ref):
        for c in range(2):
            pair_t = ref[:, c * LANE:(c + 1) * LANE].T
            heads_t.append(pair_t[:IDX_DH])
            heads_t.append(pair_t[IDX_DH:])
    rhs_all = jnp.concatenate(heads_t, axis=1).astype(BF16)
    w_t = ikw_ref[...].T
    idx_scale = (IDX_DH ** -0.5) * (IDX_HEADS ** -0.5)

    def score_blk(kb, carry):
        k0 = pl.multiple_of(kb * tki, tki)
        ikb = ik_ref[pl.ds(k0, tki), :][:, :IDX_DH]
        lg = _dot(ikb, rhs_all)
        acc = None
        for h in range(IDX_HEADS):
            term = w_t[IDX_DH + h:IDX_DH + h + 1, :] * jnp.maximum(lg[:, h * tq:(h + 1) * tq], 0.0)
            acc = term if acc is None else acc + term
        acc = acc * idx_scale
        s_idx = k0 + lax.broadcasted_iota(I32, (tki, 1), 0)
        acc = jnp.where(s_idx <= col_t, acc, -jnp.inf)
        bits = pltpu.bitcast(acc, I32)
        sc_ref[pl.ds(k0, tki), :] = jnp.where(bits < 0, bits ^ 0x7FFFFFFF, bits)
        return carry

    lax.fori_loop(0, nkb, score_blk, 0)

    def count_ge(cand):
        def body(kb, cnt):
            k0 = pl.multiple_of(kb * tki, tki)
            c = jnp.where(sc_ref[pl.ds(k0, tki), :] >= cand, 1, 0)
            return cnt + jnp.sum(c.reshape(tki // 8, 8, tq), axis=0)
        cnt = lax.fori_loop(0, nkb, body, jnp.zeros((8, tq), I32))
        return jnp.sum(cnt, axis=0, keepdims=True)

    base = jnp.where(count_ge(jnp.zeros((1, tq), I32)) >= topk, 0, INT_MIN).astype(I32)

    def bit_body(i, base):
        cand = base | jnp.left_shift(jnp.int32(1), 30 - i)
        return jnp.where(count_ge(cand) >= topk, cand, base)

    tau = lax.fori_loop(0, 31, bit_body, base)

    qg = qg_ref[...]
    scale = ATT_DH ** -0.5
    tkf = min(FAR_TK, tki)
    fsub = tkf // LANE
    nfar = jnp.maximum(qi - 1, 0) // fsub

    for h in range(ATT_HEADS):
        qt_ref[h] = (_rms(aq_ref[:, h * ATT_DH:(h + 1) * ATT_DH]) * qg).T.astype(BF16)
    acc_ref[...] = jnp.zeros_like(acc_ref)

    def softmax_block(carry, k0, width, sel, bias_of):
        m_all, l_all = carry
        kblk = kn_ref[pl.ds(k0, width), :]
        vblk = vt_ref[:, pl.ds(k0, width)]
        m_rows, l_rows = [], []
        for h in range(ATT_HEADS):
            lg = jnp.where(sel, _dot(kblk, qt_ref[h]) * scale + bias_of(h), NEG_BIG)
            m_i = m_all[h:h + 1]
            m_new = jnp.maximum(m_i, jnp.max(lg, axis=0, keepdims=True))
            alpha = jnp.exp(m_i - m_new)
            p = jnp.exp(lg - m_new)
            m_rows.append(m_new)
            l_rows.append(alpha * l_all[h:h + 1] + jnp.sum(p, axis=0, keepdims=True))
            acc_ref[h] = alpha * acc_ref[h] + _dot(vblk, p.astype(BF16))
        return jnp.concatenate(m_rows, axis=0), jnp.concatenate(l_rows, axis=0)

    def far_blk(kb, carry):
        k0 = pl.multiple_of(kb * tkf, tkf)
        sel = sc_ref[pl.ds(k0, tkf), :] >= tau
        return softmax_block(carry, k0, tkf, sel,
                             lambda h: jnp.concatenate([bias_ref[2, h]] * fsub, axis=0))

    def near_blk(j, carry):
        k0 = pl.multiple_of(j * LANE, LANE)
        s_idx = k0 + lax.broadcasted_iota(I32, (LANE, 1), 0)
        sel = jnp.logical_and(sc_ref[pl.ds(k0, LANE), :] >= tau, s_idx <= col_t)
        d = jnp.minimum(qi - j, 2)
        return softmax_block(carry, k0, LANE, sel, lambda h: bias_ref[d, h])

    init = (jnp.full((ATT_HEADS, tq), M_INIT, F32), jnp.zeros((ATT_HEADS, tq), F32))
    carry = lax.fori_loop(0, nfar, far_blk, init)
    _, l_f = lax.fori_loop(nfar * fsub, qi + 1, near_blk, carry)
    for h in range(ATT_HEADS):
        o_ref[:, h * ATT_DH:(h + 1) * ATT_DH] = (acc_ref[h] / l_f[h:h + 1]).T.astype(o_ref.dtype)


def _t5_bucket(dist):
    n = jnp.maximum(dist, 0)
    max_exact = REL_BUCKETS // 2
    large = max_exact + (jnp.log(jnp.maximum(n, 1).astype(F32) / max_exact)
                         / math.log(REL_MAX_DIST / max_exact) * (REL_BUCKETS - max_exact)).astype(I32)
    large = jnp.minimum(large, REL_BUCKETS - 1)
    return jnp.where(n < max_exact, n, large)


def _sparse_attention(proj, kn, vt, ikb, att_q_g, rel_bias, bsz, seq):
    tq = LANE
    m = bsz * seq
    nq = seq // tq
    topk = min(TOPK_MAX, seq // 4)
    tki = min(IDX_TK, seq)
    kk = jnp.arange(LANE, dtype=I32)[:, None]
    qq = jnp.arange(tq, dtype=I32)[None, :]
    dist = jnp.stack([d * LANE + qq - kk for d in range(3)])
    onehot = (_t5_bucket(dist)[..., None] == jnp.arange(REL_BUCKETS, dtype=I32)).astype(F32)
    bias = jnp.einsum("dkqb,bh->dhkq", onehot, rel_bias.astype(F32), precision=lax.Precision.HIGHEST)
    qw = ATT_HEADS * ATT_DH
    row = lambda b, i: b * nq + i
    iq_spec = lambda p: pl.BlockSpec((tq, 2 * LANE), lambda b, i: (row(b, i), OFF_IQ // (2 * LANE) + p))
    kspec = pl.BlockSpec((seq, LANE), lambda b, i: (b, 0))
    return pl.pallas_call(
        functools.partial(_att_kernel, topk=topk, tq=tq, tki=tki),
        out_shape=jax.ShapeDtypeStruct((m, qw), BF16),
        grid=(bsz, nq),
        in_specs=[
            pl.BlockSpec((tq, qw), lambda b, i: (row(b, i), OFF_AQ // qw)),
            iq_spec(0), iq_spec(1), iq_spec(2), iq_spec(3),
            pl.BlockSpec((tq, LANE), lambda b, i: (row(b, i), OFF_IK // LANE)),
            kspec,
            pl.BlockSpec((LANE, seq), lambda b, i: (0, b)),
            kspec,
            pl.BlockSpec((1, ATT_DH), lambda b, i: (0, 0)),
            pl.BlockSpec((3, ATT_HEADS, LANE, tq), lambda b, i: (0, 0, 0, 0)),
        ],
        out_specs=pl.BlockSpec((tq, qw), lambda b, i: (row(b, i), 0)),
        scratch_shapes=[pltpu.VMEM((seq, tq), I32),
                        pltpu.VMEM((ATT_HEADS, ATT_DH, tq), BF16),
                        pltpu.VMEM((ATT_HEADS, ATT_DH, tq), F32)],
        compiler_params=_cparams(("arbitrary", "arbitrary")),
        name="sparse_attention",
    )(proj, proj, proj, proj, proj, proj, kn, vt, ikb, att_q_g.reshape(1, ATT_DH), bias)


MQ_WIN = 3 * LANE


def _mem_kernel(w0_ref, w1_ref, w2_ref, kv_ref, qg_ref, kg_ref, o_ref):
    win = jnp.concatenate([w0_ref[...], w1_ref[...], w2_ref[...]], axis=1)
    mq = win[:, MQ_SHIFT:MQ_SHIFT + MEM_HEADS * MEM_DH]
    qg = qg_ref[...]
    kg = kg_ref[...]
    scale = MEM_DH ** -0.5
    kvw = MEM_HEADS * MEM_DH
    for h in range(MEM_HEADS):
        q = (_rms(mq[:, h * MEM_DH:(h + 1) * MEM_DH]) * qg).astype(BF16)
        k = (_rms(kv_ref[:, h * MEM_DH:(h + 1) * MEM_DH]) * kg).astype(BF16)
        v = kv_ref[:, kvw + h * MEM_DH:kvw + (h + 1) * MEM_DH].astype(BF16)
        lg = _dot_nt(q, k) * scale
        p = jnp.exp(lg - jnp.max(lg, axis=-1, keepdims=True))
        p = p / jnp.sum(p, axis=-1, keepdims=True)
        o_ref[:, h * MEM_DH:(h + 1) * MEM_DH] = _dot(p.astype(BF16), v).astype(o_ref.dtype)


def _memory_attention(proj, kv, mem_q_g, mem_k_g, bsz, seq, ts):
    ts = min(ts, seq)
    ns = seq // ts
    m = bsz * seq
    qw = MEM_HEADS * MEM_DH
    row = lambda b, i: b * ns + i
    wspec = lambda p: pl.BlockSpec((ts, MQ_WIN), lambda b, i: (row(b, i), OFF_IK // MQ_WIN + p))
    return pl.pallas_call(
        _mem_kernel,
        out_shape=jax.ShapeDtypeStruct((m, qw), BF16),
        grid=(bsz, ns),
        in_specs=[wspec(0), wspec(1), wspec(2),
                  pl.BlockSpec((MEM_LEN, 2 * qw), lambda b, i: (b, 0)),
                  pl.BlockSpec((1, MEM_DH), lambda b, i: (0, 0)),
                  pl.BlockSpec((1, MEM_DH), lambda b, i: (0, 0))],
        out_specs=pl.BlockSpec((ts, qw), lambda b, i: (row(b, i), 0)),
        compiler_params=_cparams(("arbitrary", "arbitrary")),
        name="memory_attention",
    )(proj, proj, proj, kv, mem_q_g.reshape(1, MEM_DH), mem_k_g.reshape(1, MEM_DH))


MERGE_TN = 256


def _merge_kernel(yr_ref, ya_ref, ym_ref, wr_ref, wa_ref, wm_ref,
                  g0_ref, e0_ref, g1_ref, e1_ref, g2_ref, e2_ref, o_ref):
    acc = None
    for y_ref, w_ref, g_ref, e_ref in ((yr_ref, wr_ref, g0_ref, e0_ref),
                                       (ya_ref, wa_ref, g1_ref, e1_ref),
                                       (ym_ref, wm_ref, g2_ref, e2_ref)):
        g = jnp.concatenate([g_ref[...], e_ref[...]], axis=1)[:, GATE_SHIFT:GATE_SHIFT + MERGE_TN]
        y = _dot(y_ref[...], w_ref[...].astype(BF16))
        term = y / (1.0 + jnp.exp(-g))
        acc = term if acc is None else acc + term
    o_ref[...] = acc.astype(o_ref.dtype)


def _merge(proj, y_ret, y_att, y_mem, w_up_ret, w_up_att, w_up_mem, d_model, tm):
    m = y_ret.shape[0]
    tm = min(tm, m)
    tn = MERGE_TN
    yspec = lambda y: pl.BlockSpec((tm, y.shape[1]), lambda i, n: (i, 0), pipeline_mode=pl.Buffered(1))
    wspec = lambda w: pl.BlockSpec((w.shape[0], tn), lambda i, n: (0, n))
    gspecs = []
    for br in range(3):
        base = GATE_BASE + br * d_model
        gspecs.append(pl.BlockSpec((tm, tn), lambda i, n, base=base: (i, base // tn + n)))
        gspecs.append(pl.BlockSpec((tm, LANE), lambda i, n, base=base: (i, base // LANE + (n + 1) * (tn // LANE))))
    return pl.pallas_call(
        _merge_kernel,
        out_shape=jax.ShapeDtypeStruct((m, d_model), BF16),
        grid=(m // tm, d_model // tn),
        in_specs=[yspec(y_ret), yspec(y_att), yspec(y_mem),
                  wspec(w_up_ret), wspec(w_up_att), wspec(w_up_mem)] + gspecs,
        out_specs=pl.BlockSpec((tm, tn), lambda i, n: (i, n)),
        compiler_params=_cparams(("arbitrary", "arbitrary")),
        name="gated_merge",
    )(y_ret, y_att, y_mem, w_up_ret, w_up_att, w_up_mem, *([proj] * 6))


RANK_NONE = 64.0


def _top_rows(vals, k, with_rank=False):
    rows = []
    rank = jnp.full(vals.shape, RANK_NONE, F32) if with_rank else None
    for r in range(k):
        mx = jnp.max(vals, axis=0, keepdims=True)
        rows.append(mx)
        top = vals == mx
        if with_rank:
            rank = jnp.where(top, r + 1.0, rank)
        vals = jnp.where(top, -jnp.inf, vals)
    return (rows, rank) if with_rank else rows


def _router_kernel(q_ref, keys_ref, u_ref, v_ref, c0_ref, e0_ref, r1_ref, e1_ref, ub_ref, vb_ref):
    ub_ref[...] = u_ref[...].astype(ub_ref.dtype)
    vb_ref[...] = v_ref[...].astype(vb_ref.dtype)
    kk = PEER_TOPK
    s_t = []
    for c in range(2):
        qs = q_ref[:, c * LANE:(c + 1) * LANE].astype(BF16)
        s_t.append(_dot_nt(keys_ref[c].astype(BF16), qs))
    a = _top_rows(s_t[0], kk)
    b, rank1 = _top_rows(s_t[1], kk, with_rank=True)
    amat = jnp.concatenate(a, axis=0)
    bmat = jnp.concatenate(b, axis=0)
    half = kk // 2
    cand = jnp.concatenate([a[0] + bmat] + [a[r] + bmat[:half] for r in range(1, half)]
                           + [amat[half:] + b[0]], axis=0)
    tv = _top_rows(cand, kk)
    tau = tv[kk - 1]
    z = None
    for r in range(kk):
        e = jnp.exp(tv[r] - tv[0])
        z = e if z is None else z + e
    cnt = jnp.zeros(amat.shape, F32)
    for j in range(kk):
        cnt = cnt + jnp.where(amat + b[j] >= tau, 1.0, 0.0)
    count0 = jnp.zeros(s_t[0].shape, F32)
    for r in range(kk):
        count0 = jnp.where(s_t[0] == a[r], cnt[r:r + 1], count0)
    c0_ref[0] = count0
    e0_ref[0] = jnp.exp(s_t[0] - a[0]) * (1.0 / z)
    r1_ref[0] = rank1
    e1_ref[0] = jnp.exp(s_t[1] - b[0])


def _peer_router(q, sub_keys, u_tab, v_tab, tt):
    m = q.shape[0]
    tt = min(tt, m)
    nk = PEER_NKEYS
    ne, d = u_tab.shape
    steps = (m // tt) * PEER_HEADS
    slab = ne // steps
    assert slab * steps == ne and slab % 16 == 0
    keys = sub_keys.reshape(PEER_HEADS * 2, nk, PEER_DQ // 2)
    f_shape = jax.ShapeDtypeStruct((PEER_HEADS, nk, m), F32)
    t_shape = jax.ShapeDtypeStruct((ne, d), BF16)
    s_spec = pl.BlockSpec((1, nk, tt), lambda i, h: (h, 0, i))
    t_spec = pl.BlockSpec((slab, d), lambda i, h: (i * PEER_HEADS + h, 0))
    return pl.pallas_call(
        _router_kernel,
        out_shape=(f_shape,) * 4 + (t_shape,) * 2,
        grid=(m // tt, PEER_HEADS),
        in_specs=[pl.BlockSpec((tt, PEER_DQ), lambda i, h: (i, h)),
                  pl.BlockSpec((2, nk, PEER_DQ // 2), lambda i, h: (h, 0, 0)),
                  t_spec, t_spec],
        out_specs=(s_spec,) * 4 + (t_spec,) * 2,
        compiler_params=_cparams(("arbitrary", "arbitrary")),
        name="peer_router",
    )(q, keys, u_tab, v_tab)


def _gelu(x):
    return 0.5 * x * (1.0 + lax.erf(x * (2.0 ** -0.5)))


def _peer_gate_kernel(h_ref, u_ref, c0_ref, e0_ref, r1_ref, e1_ref, o_ref, z_ref, *, te):
    e = pl.program_id(1)
    nsub = te // PEER_NKEYS
    tm = h_ref.shape[0]
    z_ref[...] = _dot_nt(u_ref[...], h_ref[...])
    for ii in range(nsub):
        i_idx = e * nsub + ii
        rows = slice(ii * PEER_NKEYS, (ii + 1) * PEER_NKEYS)
        c0_rows = [c0_ref[h, pl.ds(i_idx, 1), :] for h in range(PEER_HEADS)]
        e0_rows = [e0_ref[h, pl.ds(i_idx, 1), :] for h in range(PEER_HEADS)]
        for tc in range(tm // LANE):
            cols = slice(tc * LANE, (tc + 1) * LANE)
            w = None
            for h in range(PEER_HEADS):
                hit = r1_ref[h, :, cols] <= c0_rows[h][:, cols]
                term = jnp.where(hit, e1_ref[h, :, cols] * e0_rows[h][:, cols], 0.0)
                w = term if w is None else w + term
            p_t = _gelu(z_ref[rows, cols]) * w
            o_ref[cols, rows] = p_t.T.astype(o_ref.dtype)


def _peer_gate(h2, u_bf, c0t, e0t, r1t, e1t, tm, te):
    m, d = h2.shape
    ne = u_bf.shape[0]
    tm = min(tm, m)
    sspec = pl.BlockSpec((PEER_HEADS, PEER_NKEYS, tm), lambda i, e: (0, 0, i))
    return pl.pallas_call(
        functools.partial(_peer_gate_kernel, te=te),
        out_shape=jax.ShapeDtypeStruct((m, ne), BF16),
        grid=(m // tm, ne // te),
        in_specs=[pl.BlockSpec((tm, d), lambda i, e: (i, 0)),
                  pl.BlockSpec((te, d), lambda i, e: (e, 0)),
                  sspec, sspec, sspec, sspec],
        out_specs=pl.BlockSpec((tm, te), lambda i, e: (i, e)),
        scratch_shapes=[pltpu.VMEM((te, tm), F32)],
        compiler_params=_cparams(("arbitrary", "arbitrary")),
        name="peer_gate",
    )(h2, u_bf, c0t, e0t, r1t, e1t)


def _peer_out_kernel(p_ref, v_ref, x_ref, o_ref):
    @pl.when(pl.program_id(2) == 0)
    def _():
        o_ref[...] = x_ref[...]

    o_ref[...] += _dot(p_ref[...], v_ref[...])


def _peer_out(p, v_bf, x, tm, tn, tk):
    m, ne = p.shape
    d = v_bf.shape[1]
    tm, tn, tk = min(tm, m), min(tn, d), min(tk, ne)
    return pl.pallas_call(
        _peer_out_kernel,
        out_shape=jax.ShapeDtypeStruct((m, d), F32),
        grid=(m // tm, d // tn, ne // tk),
        in_specs=[pl.BlockSpec((tm, tk), lambda i, n, k: (i, k)),
                  pl.BlockSpec((tk, tn), lambda i, n, k: (k, n)),
                  pl.BlockSpec((tm, tn), lambda i, n, k: (i, n))],
        out_specs=pl.BlockSpec((tm, tn), lambda i, n, k: (i, n)),
        compiler_params=_cparams(("arbitrary", "arbitrary", "arbitrary")),
        name="peer_out",
    )(p, v_bf, x)


def kernel(x, mem, positions, ln1_g, w_in, att_q_g, att_k_g, rel_bias, mem_g, w_mem_kv, mem_q_g, mem_k_g,
           w_up_ret, w_up_att, w_up_mem, w_out, ln2_g, peer_w_q, peer_sub_keys, peer_u, peer_v):
    bsz, seq, d = x.shape
    m = bsz * seq
    depth = ln1_g.shape[0]
    xf = x.reshape(m, d)
    memf = mem.reshape(bsz * mem.shape[1], d)
    t = Tiles
    for l in range(depth):
        h = _rmsnorm(xf, ln1_g[l], t.NORM_ROWS)
        proj = _matmul_ring(h, jnp.swapaxes(w_in[l], 0, 1), tm=t.IN_PROJ[0], tn=t.IN_PROJ[1],
                            w_transposed=True, name="in_proj")
        y_ret = _retention(proj, positions, bsz, seq)
        kn, vt, ikb = _kprep(proj, att_k_g[l], t.KPREP_ROWS)
        y_att = _sparse_attention(proj, kn, vt, ikb, att_q_g[l], rel_bias, bsz, seq)
        kv = _matmul(_rmsnorm(memf, mem_g[l], t.NORM_ROWS), w_mem_kv[l],
                     tm=t.MEM_KV[0], tn=t.MEM_KV[1], name="mem_kv")
        y_mem = _memory_attention(proj, kv, mem_q_g[l], mem_k_g[l], bsz, seq, t.MEM_ATT_ROWS)
        merged = _merge(proj, y_ret, y_att, y_mem, w_up_ret[l], w_up_att[l], w_up_mem[l], d, t.MERGE_ROWS)
        xf = _matmul_ring(merged, w_out[l], tm=t.OUT_PROJ[0], tn=t.OUT_PROJ[1], residual=xf, name="out_proj")
        h2 = _rmsnorm(xf, ln2_g[l], t.NORM_ROWS)
        q = _matmul(h2, peer_w_q[l], tm=t.PEER_QUERY[0], tn=t.PEER_QUERY[1], name="peer_query")
        c0t, e0t, r1t, e1t, u_bf, v_bf = _peer_router(q, peer_sub_keys[l], peer_u[l], peer_v[l],
                                                      t.ROUTER_TOKENS)
        p = _peer_gate(h2, u_bf, c0t, e0t, r1t, e1t, *t.GATE)
        xf = _peer_out(p, v_bf, xf, *t.PEER_OUT)
    return xf.reshape(bsz, seq, d)
```

```python
import functools
import math

import jax
import jax.numpy as jnp
from jax import lax
from jax.experimental import pallas as pl
from jax.experimental.pallas import tpu as pltpu

F32 = jnp.float32
BF16 = jnp.bfloat16
I32 = jnp.int32

MEM_LEN = 256
RET_HEADS, RET_DK, RET_DV, RET_CHUNK = 8, 128, 256, 128
ATT_HEADS, ATT_DH = 8, 128
IDX_HEADS, IDX_DH = 16, 64
TOPK_MAX = 256
MEM_HEADS, MEM_DH = 4, 256
REL_BUCKETS, REL_MAX_DIST = 32, 128
PEER_HEADS, PEER_NKEYS, PEER_DQ, PEER_TOPK = 8, 128, 256, 16
EPS = 1e-6
ROPE_BASE = 10000.0

OFF_RQ = 0
OFF_RK = OFF_RQ + RET_HEADS * RET_DK
OFF_RV = OFF_RK + RET_HEADS * RET_DK
OFF_RG = OFF_RV + RET_HEADS * RET_DV
OFF_AQ = OFF_RG + RET_HEADS * RET_DV
OFF_AK = OFF_AQ + ATT_HEADS * ATT_DH
OFF_AV = OFF_AK + ATT_DH
OFF_IQ = OFF_AV + ATT_DH
OFF_IK = OFF_IQ + IDX_HEADS * IDX_DH
OFF_IW = OFF_IK + IDX_DH
OFF_MQ = OFF_IW + IDX_HEADS
OFF_GATE = OFF_MQ + MEM_HEADS * MEM_DH

LANE = 128
GATE_SHIFT = OFF_GATE % LANE
GATE_BASE = OFF_GATE - GATE_SHIFT
MQ_SHIFT = OFF_MQ - OFF_IK
VMEM_LIMIT = 56 * 1024 * 1024


class Tiles:
    NORM_ROWS = 256
    IN_PROJ = (1024, 512)
    OUT_PROJ = (1024, 512)
    PEER_QUERY = (1024, 512)
    MEM_KV = (512, 512)
    KPREP_ROWS = 512
    MEM_ATT_ROWS = 512
    MERGE_ROWS = 2048
    ROUTER_TOKENS = 1024
    GATE = (512, 1024)
    PEER_OUT = (512, 2048, 2048)

INT_MIN = -(2 ** 31)
NEG_BIG = -1e30


def _cparams(sem):
    return pltpu.CompilerParams(dimension_semantics=sem, vmem_limit_bytes=VMEM_LIMIT)


def _dot_nt(a, b):
    return lax.dot_general(a, b, (((1,), (1,)), ((), ())), preferred_element_type=F32)


def _dot(a, b):
    return jnp.dot(a, b, preferred_element_type=F32)


def _rms(x):
    return x * lax.rsqrt(jnp.mean(x * x, axis=-1, keepdims=True) + EPS)


def _rmsnorm_kernel(x_ref, g_ref, o_ref):
    o_ref[...] = (_rms(x_ref[...].astype(F32)) * g_ref[...]).astype(o_ref.dtype)


def _rmsnorm(x, g, tm):
    m, d = x.shape
    tm = min(tm, m)
    return pl.pallas_call(
        _rmsnorm_kernel,
        out_shape=jax.ShapeDtypeStruct((m, d), BF16),
        grid=(m // tm,),
        in_specs=[pl.BlockSpec((tm, d), lambda i: (i, 0)),
                  pl.BlockSpec((1, d), lambda i: (0, 0))],
        out_specs=pl.BlockSpec((tm, d), lambda i: (i, 0)),
        compiler_params=_cparams(("arbitrary",)),
        name="rmsnorm",
    )(x, g.reshape(1, d))


def _mm_kernel(*refs, has_res, w_transposed):
    if has_res:
        a_ref, w_ref, r_ref, o_ref, wsc = refs
    else:
        a_ref, w_ref, o_ref, wsc = refs

    @pl.when(pl.program_id(1) == 0)
    def _():
        wsc[...] = w_ref[...].astype(BF16)

    acc = _dot_nt(a_ref[...], wsc[...]) if w_transposed else _dot(a_ref[...], wsc[...])
    if has_res:
        acc = acc + r_ref[...]
    o_ref[...] = acc.astype(o_ref.dtype)


def _matmul(a, w, *, tm, tn, residual=None, out_dtype=F32, w_transposed=False, name="matmul"):
    m, k = a.shape
    n = w.shape[0] if w_transposed else w.shape[1]
    tm = min(tm, m)
    tn = min(tn, n)
    if w_transposed:
        w_spec = pl.BlockSpec((tn, k), lambda j, i: (j, 0))
        w_scratch = pltpu.VMEM((tn, k), BF16)
    else:
        w_spec = pl.BlockSpec((k, tn), lambda j, i: (0, j))
        w_scratch = pltpu.VMEM((k, tn), BF16)
    in_specs = [pl.BlockSpec((tm, k), lambda j, i: (i, 0)), w_spec]
    args = [a, w]
    if residual is not None:
        in_specs.append(pl.BlockSpec((tm, tn), lambda j, i: (i, j)))
        args.append(residual)
    return pl.pallas_call(
        functools.partial(_mm_kernel, has_res=residual is not None, w_transposed=w_transposed),
        out_shape=jax.ShapeDtypeStruct((m, n), out_dtype),
        grid=(pl.cdiv(n, tn), m // tm),
        in_specs=in_specs,
        out_specs=pl.BlockSpec((tm, tn), lambda j, i: (i, j)),
        scratch_shapes=[w_scratch],
        compiler_params=_cparams(("arbitrary", "arbitrary")),
        name=name,
    )(*args)


A_RING = 3


def _mm_ring_kernel(*refs, has_res, w_transposed, tm, ni, nsteps):
    if has_res:
        a_hbm, w_ref, r_ref, o_ref, abuf, sem, wsc = refs
    else:
        a_hbm, w_ref, o_ref, abuf, sem, wsc = refs
    i = pl.program_id(1)
    s = pl.program_id(0) * ni + i

    def a_copy(step, slot):
        row = pl.multiple_of(lax.rem(step, ni) * tm, tm)
        return pltpu.make_async_copy(a_hbm.at[pl.ds(row, tm), :], abuf.at[slot], sem.at[slot])

    @pl.when(s == 0)
    def _():
        a_copy(0, 0).start()
        a_copy(1, 1).start()

    @pl.when(s + 2 < nsteps)
    def _():
        a_copy(s + 2, lax.rem(s + 2, A_RING)).start()

    @pl.when(i == 0)
    def _():
        wsc[...] = w_ref[...].astype(BF16)

    slot = lax.rem(s, A_RING)
    a_copy(s, slot).wait()
    acc = _dot_nt(abuf[slot], wsc[...]) if w_transposed else _dot(abuf[slot], wsc[...])
    if has_res:
        acc = acc + r_ref[...]
    o_ref[...] = acc.astype(o_ref.dtype)


def _matmul_ring(a, w, *, tm, tn, residual=None, out_dtype=F32, w_transposed=False, name="matmul_ring"):
    m, k = a.shape
    n = w.shape[0] if w_transposed else w.shape[1]
    tm = min(tm, m)
    tn = min(tn, n)
    ni = m // tm
    nj = pl.cdiv(n, tn)
    assert ni * nj >= 2
    w_block = (tn, k) if w_transposed else (k, tn)
    w_index = (lambda j, i: (j, 0)) if w_transposed else (lambda j, i: (0, j))
    in_specs = [pl.BlockSpec(memory_space=pl.ANY), pl.BlockSpec(w_block, w_index)]
    args = [a, w]
    if residual is not None:
        in_specs.append(pl.BlockSpec((tm, tn), lambda j, i: (i, j)))
        args.append(residual)
    return pl.pallas_call(
        functools.partial(_mm_ring_kernel, has_res=residual is not None, w_transposed=w_transposed,
                          tm=tm, ni=ni, nsteps=ni * nj),
        out_shape=jax.ShapeDtypeStruct((m, n), out_dtype),
        grid=(nj, ni),
        in_specs=in_specs,
        out_specs=pl.BlockSpec((tm, tn), lambda j, i: (i, j)),
        scratch_shapes=[pltpu.VMEM((A_RING, tm, k), BF16),
                        pltpu.SemaphoreType.DMA((A_RING,)),
                        pltpu.VMEM(w_block, BF16)],
        compiler_params=_cparams(("arbitrary", "arbitrary")),
        name=name,
    )(*args)


def _ret_kernel(cdec_ref, rq_ref, rk_ref, rv_ref, rg_ref, pos_ref, invf_ref, sign_ref,
                idec_ref, qdec_ref, kdec_ref, o_ref, st_ref):
    @pl.when(pl.program_id(1) == 0)
    def _():
        st_ref[...] = jnp.zeros_like(st_ref)

    ang = pos_ref[...].astype(F32) * invf_ref[...]
    cos = jnp.cos(ang)
    sin = jnp.sin(ang) * sign_ref[...]
    scale = RET_DK ** -0.5
    for h in range(RET_HEADS):
        q = rq_ref[:, h * RET_DK:(h + 1) * RET_DK]
        k = rk_ref[:, h * RET_DK:(h + 1) * RET_DK]
        q = q * cos + pltpu.roll(q, RET_DK // 2, 1) * sin
        k = (k * cos + pltpu.roll(k, RET_DK // 2, 1) * sin) * scale
        vb = rv_ref[:, h * RET_DV:(h + 1) * RET_DV].astype(BF16)
        s = _dot_nt(q.astype(BF16), k.astype(BF16)) * idec_ref[h]
        inner = _dot(s.astype(BF16), vb)
        st = st_ref[h]
        cross = _dot((q * qdec_ref[h]).astype(BF16), st.astype(BF16))
        kd = (k * kdec_ref[h]).astype(BF16)
        upd = lax.dot_general(kd, vb, (((0,), (0,)), ((), ())), preferred_element_type=F32)
        st_ref[h] = st * cdec_ref[h] + upd
        y = _rms(inner + cross)
        g = rg_ref[:, h * RET_DV:(h + 1) * RET_DV]
        o_ref[:, h * RET_DV:(h + 1) * RET_DV] = (g / (1.0 + jnp.exp(-g)) * y).astype(o_ref.dtype)


def _retention(proj, positions, bsz, seq):
    c = RET_CHUNK
    nc = seq // c
    m = bsz * seq
    hh = RET_HEADS
    log_gamma = jnp.log1p(-jnp.exp2(-5.0 - jnp.arange(hh, dtype=F32)))
    idx = jnp.arange(c, dtype=F32)
    diff = idx[:, None] - idx[None, :]
    idec = jnp.where(diff[None] >= 0,
                     jnp.exp(log_gamma[:, None, None] * jnp.maximum(diff, 0.0)[None]), 0.0)
    qdec = jnp.exp(log_gamma[:, None] * (idx + 1.0)[None])
    kdec = jnp.exp(log_gamma[:, None] * (c - 1.0 - idx)[None])
    cdec = jnp.exp(log_gamma * c)
    qdec_b = jnp.broadcast_to(qdec[:, :, None], (hh, c, RET_DK))
    kdec_b = jnp.broadcast_to(kdec[:, :, None], (hh, c, RET_DK))
    half = RET_DK // 2
    inv_freq = 1.0 / (ROPE_BASE ** (jnp.arange(half, dtype=F32) / half))
    invf = jnp.concatenate([inv_freq, inv_freq]).reshape(1, RET_DK)
    sign = jnp.concatenate([-jnp.ones((half,), F32), jnp.ones((half,), F32)]).reshape(1, RET_DK)
    qk_w = hh * RET_DK
    v_w = hh * RET_DV
    row = lambda b, i: b * nc + i
    const3 = lambda b, i: (0, 0, 0)
    return pl.pallas_call(
        _ret_kernel,
        out_shape=jax.ShapeDtypeStruct((m, v_w), BF16),
        grid=(bsz, nc),
        in_specs=[
            pl.BlockSpec(memory_space=pltpu.SMEM),
            pl.BlockSpec((c, qk_w), lambda b, i: (row(b, i), OFF_RQ // qk_w)),
            pl.BlockSpec((c, qk_w), lambda b, i: (row(b, i), OFF_RK // qk_w)),
            pl.BlockSpec((c, v_w), lambda b, i: (row(b, i), OFF_RV // v_w)),
            pl.BlockSpec((c, v_w), lambda b, i: (row(b, i), OFF_RG // v_w)),
            pl.BlockSpec((c, 1), lambda b, i: (row(b, i), 0)),
            pl.BlockSpec((1, RET_DK), lambda b, i: (0, 0)),
            pl.BlockSpec((1, RET_DK), lambda b, i: (0, 0)),
            pl.BlockSpec((hh, c, c), const3),
            pl.BlockSpec((hh, c, RET_DK), const3),
            pl.BlockSpec((hh, c, RET_DK), const3),
        ],
        out_specs=pl.BlockSpec((c, v_w), lambda b, i: (row(b, i), 0)),
        scratch_shapes=[pltpu.VMEM((hh, RET_DK, RET_DV), F32)],
        compiler_params=_cparams(("arbitrary", "arbitrary")),
        name="retention",
    )(cdec, proj, proj, proj, proj, positions.reshape(m, 1), invf, sign, idec, qdec_b, kdec_b)


def _kprep_kernel(ak_ref, av_ref, ikw_ref, kg_ref, kn_ref, vt_ref, ik_ref):
    kn_ref[...] = (_rms(ak_ref[...]) * kg_ref[...]).astype(BF16)
    vt_ref[...] = av_ref[...].T.astype(BF16)
    ik_ref[...] = ikw_ref[...].astype(BF16)


def _kprep(proj, att_k_g, tm):
    m = proj.shape[0]
    tm = min(tm, m)
    blk = lambda off: pl.BlockSpec((tm, LANE), lambda i: (i, off // LANE))
    nat = jax.ShapeDtypeStruct((m, LANE), BF16)
    nat_spec = pl.BlockSpec((tm, LANE), lambda i: (i, 0))
    return pl.pallas_call(
        _kprep_kernel,
        out_shape=(nat, jax.ShapeDtypeStruct((LANE, m), BF16), nat),
        grid=(m // tm,),
        in_specs=[blk(OFF_AK), blk(OFF_AV), blk(OFF_IK),
                  pl.BlockSpec((1, LANE), lambda i: (0, 0))],
        out_specs=(nat_spec, pl.BlockSpec((LANE, tm), lambda i: (0, i)), nat_spec),
        compiler_params=_cparams(("arbitrary",)),
        name="att_kprep",
    )(proj, proj, proj, att_k_g.reshape(1, ATT_DH))


IDX_TK = 512
FAR_TK = 512
M_INIT = -1e29


def _att_kernel(aq_ref, iq0_ref, iq1_ref, iq2_ref, iq3_ref, ikw_ref, kn_ref, vt_ref, ik_ref,
                qg_ref, bias_ref, o_ref, sc_ref, qt_ref, acc_ref, *, topk, tq, tki):
    qi = pl.program_id(1)
    t0 = qi * tq
    nkb = (t0 + tq + tki - 1) // tki
    col_t = t0 + lax.broadcasted_iota(I32, (1, tq), 1)
    sub = tki // LANE

    heads_t = []
    for ref in (iq0_ref, iq1_ref, iq2_ref, iq3_ref):
        for c in range(2):
            pair_t = ref[:, c * LANE:(c + 1) * LANE].T
            heads_t.append(pair_t[:IDX_DH])
            heads_t.append(pair_t[IDX_DH:])
    rhs_all = jnp.concatenate(heads_t, axis=1).astype(BF16)
    w_t = ikw_ref[...].T
    idx_scale = (IDX_DH ** -0.5) * (IDX_HEADS ** -0.5)

    def score_blk(kb, carry):
        k0 = pl.multiple_of(kb * tki, tki)
        ikb = ik_ref[pl.ds(k0, tki), :][:, :IDX_DH]
        lg = _dot(ikb, rhs_all)
        acc = None
        for h in range(IDX_HEADS):
            term = w_t[IDX_DH + h:IDX_DH + h + 1, :] * jnp.maximum(lg[:, h * tq:(h + 1) * tq], 0.0)
            acc = term if acc is None else acc + term
        acc = acc * idx_scale
        s_idx = k0 + lax.broadcasted_iota(I32, (tki, 1), 0)
        acc = jnp.where(s_idx <= col_t, acc, -jnp.inf)
        bits = pltpu.bitcast(acc, I32)
        sc_ref[pl.ds(k0, tki), :] = jnp.where(bits < 0, bits ^ 0x7FFFFFFF, bits)
        return carry

    lax.fori_loop(0, nkb, score_blk, 0)

    def count_ge(cand):
        def body(kb, cnt):
            k0 = pl.multiple_of(kb * tki, tki)
            c = jnp.where(sc_ref[pl.ds(k0, tki), :] >= cand, 1, 0)
            return cnt + jnp.sum(c.reshape(tki // 8, 8, tq), axis=0)
        cnt = lax.fori_loop(0, nkb, body, jnp.zeros((8, tq), I32))
        return jnp.sum(cnt, axis=0, keepdims=True)

    base = jnp.where(count_ge(jnp.zeros((1, tq), I32)) >= topk, 0, INT_MIN).astype(I32)

    def bit_body(i, base):
        cand = base | jnp.left_shift(jnp.int32(1), 30 - i)
        return jnp.where(count_ge(cand) >= topk, cand, base)

    tau = lax.fori_loop(0, 31, bit_body, base)

    qg = qg_ref[...]
    scale = ATT_DH ** -0.5
    tkf = min(FAR_TK, tki)
    fsub = tkf // LANE
    nfar = jnp.maximum(qi - 1, 0) // fsub

    for h in range(ATT_HEADS):
        qt_ref[h] = (_rms(aq_ref[:, h * ATT_DH:(h + 1) * ATT_DH]) * qg).T.astype(BF16)
    acc_ref[...] = jnp.zeros_like(acc_ref)

    def softmax_block(carry, k0, width, sel, bias_of=None, shift_of=None):
        m_all, l_all = carry
        kblk = kn_ref[pl.ds(k0, width), :]
        vblk = vt_ref[:, pl.ds(k0, width)]
        m_rows, l_rows = [], []
        for h in range(ATT_HEADS):
            lg = _dot(kblk, qt_ref[h]) * scale
            if bias_of is not None:
                lg = lg + bias_of(h)
            lg = jnp.where(sel, lg, NEG_BIG)
            m_i = m_all[h:h + 1]
            m_blk = jnp.max(lg, axis=0, keepdims=True)
            if shift_of is not None:
                m_blk = m_blk + shift_of(h)
            m_new = jnp.maximum(m_i, m_blk)
            alpha = jnp.exp(m_i - m_new)
            p = jnp.exp(lg - (m_new if shift_of is None else m_new - shift_of(h)))
            m_rows.append(m_new)
            l_rows.append(alpha * l_all[h:h + 1] + jnp.sum(p, axis=0, keepdims=True))
            acc_ref[h] = alpha * acc_ref[h] + _dot(vblk, p.astype(BF16))
        return jnp.concatenate(m_rows, axis=0), jnp.concatenate(l_rows, axis=0)

    def far_blk(kb, carry):
        k0 = pl.multiple_of(kb * tkf, tkf)
        sel = sc_ref[pl.ds(k0, tkf), :] >= tau
        return softmax_block(carry, k0, tkf, sel, shift_of=lambda h: bias_ref[2, h, 0:1, :])

    def near_blk(j, carry):
        k0 = pl.multiple_of(j * LANE, LANE)
        s_idx = k0 + lax.broadcasted_iota(I32, (LANE, 1), 0)
        sel = jnp.logical_and(sc_ref[pl.ds(k0, LANE), :] >= tau, s_idx <= col_t)
        d = jnp.minimum(qi - j, 2)
        return softmax_block(carry, k0, LANE, sel, bias_of=lambda h: bias_ref[d, h])

    init = (jnp.full((ATT_HEADS, tq), M_INIT, F32), jnp.zeros((ATT_HEADS, tq), F32))
    carry = lax.fori_loop(0, nfar, far_blk, init)
    _, l_f = lax.fori_loop(nfar * fsub, qi + 1, near_blk, carry)
    for h in range(ATT_HEADS):
        o_ref[:, h * ATT_DH:(h + 1) * ATT_DH] = (acc_ref[h] / l_f[h:h + 1]).T.astype(o_ref.dtype)


def _t5_bucket(dist):
    n = jnp.maximum(dist, 0)
    max_exact = REL_BUCKETS // 2
    large = max_exact + (jnp.log(jnp.maximum(n, 1).astype(F32) / max_exact)
                         / math.log(REL_MAX_DIST / max_exact) * (REL_BUCKETS - max_exact)).astype(I32)
    large = jnp.minimum(large, REL_BUCKETS - 1)
    return jnp.where(n < max_exact, n, large)


def _sparse_attention(proj, kn, vt, ikb, att_q_g, rel_bias, bsz, seq):
    tq = LANE
    m = bsz * seq
    nq = seq // tq
    topk = min(TOPK_MAX, seq // 4)
    tki = min(IDX_TK, seq)
    kk = jnp.arange(LANE, dtype=I32)[:, None]
    qq = jnp.arange(tq, dtype=I32)[None, :]
    dist = jnp.stack([d * LANE + qq - kk for d in range(3)])
    onehot = (_t5_bucket(dist)[..., None] == jnp.arange(REL_BUCKETS, dtype=I32)).astype(F32)
    bias = jnp.einsum("dkqb,bh->dhkq", onehot, rel_bias.astype(F32), precision=lax.Precision.HIGHEST)
    qw = ATT_HEADS * ATT_DH
    row = lambda b, i: b * nq + i
    iq_spec = lambda p: pl.BlockSpec((tq, 2 * LANE), lambda b, i: (row(b, i), OFF_IQ // (2 * LANE) + p))
    kspec = pl.BlockSpec((seq, LANE), lambda b, i: (b, 0))
    return pl.pallas_call(
        functools.partial(_att_kernel, topk=topk, tq=tq, tki=tki),
        out_shape=jax.ShapeDtypeStruct((m, qw), BF16),
        grid=(bsz, nq),
        in_specs=[
            pl.BlockSpec((tq, qw), lambda b, i: (row(b, i), OFF_AQ // qw)),
            iq_spec(0), iq_spec(1), iq_spec(2), iq_spec(3),
            pl.BlockSpec((tq, LANE), lambda b, i: (row(b, i), OFF_IK // LANE)),
            kspec,
            pl.BlockSpec((LANE, seq), lambda b, i: (0, b)),
            kspec,
            pl.BlockSpec((1, ATT_DH), lambda b, i: (0, 0)),
            pl.BlockSpec((3, ATT_HEADS, LANE, tq), lambda b, i: (0, 0, 0, 0)),
        ],
        out_specs=pl.BlockSpec((tq, qw), lambda b, i: (row(b, i), 0)),
        scratch_shapes=[pltpu.VMEM((seq, tq), I32),
                        pltpu.VMEM((ATT_HEADS, ATT_DH, tq), BF16),
                        pltpu.VMEM((ATT_HEADS, ATT_DH, tq), F32)],
        compiler_params=_cparams(("arbitrary", "arbitrary")),
        name="sparse_attention",
    )(proj, proj, proj, proj, proj, proj, kn, vt, ikb, att_q_g.reshape(1, ATT_DH), bias)


MQ_WIN = 3 * LANE


def _mem_kernel(w0_ref, w1_ref, w2_ref, kv_ref, qg_ref, kg_ref, o_ref):
    win = jnp.concatenate([w0_ref[...], w1_ref[...], w2_ref[...]], axis=1)
    mq = win[:, MQ_SHIFT:MQ_SHIFT + MEM_HEADS * MEM_DH]
    qg = qg_ref[...]
    kg = kg_ref[...]
    scale = MEM_DH ** -0.5
    kvw = MEM_HEADS * MEM_DH
    for h in range(MEM_HEADS):
        q = (_rms(mq[:, h * MEM_DH:(h + 1) * MEM_DH]) * qg).astype(BF16)
        k = (_rms(kv_ref[:, h * MEM_DH:(h + 1) * MEM_DH]) * kg).astype(BF16)
        v = kv_ref[:, kvw + h * MEM_DH:kvw + (h + 1) * MEM_DH].astype(BF16)
        lg = _dot_nt(q, k) * scale
        p = jnp.exp(lg - jnp.max(lg, axis=-1, keepdims=True))
        p = p / jnp.sum(p, axis=-1, keepdims=True)
        o_ref[:, h * MEM_DH:(h + 1) * MEM_DH] = _dot(p.astype(BF16), v).astype(o_ref.dtype)


def _memory_attention(proj, kv, mem_q_g, mem_k_g, bsz, seq, ts):
    ts = min(ts, seq)
    ns = seq // ts
    m = bsz * seq
    qw = MEM_HEADS * MEM_DH
    row = lambda b, i: b * ns + i
    wspec = lambda p: pl.BlockSpec((ts, MQ_WIN), lambda b, i: (row(b, i), OFF_IK // MQ_WIN + p))
    return pl.pallas_call(
        _mem_kernel,
        out_shape=jax.ShapeDtypeStruct((m, qw), BF16),
        grid=(bsz, ns),
        in_specs=[wspec(0), wspec(1), wspec(2),
                  pl.BlockSpec((MEM_LEN, 2 * qw), lambda b, i: (b, 0)),
                  pl.BlockSpec((1, MEM_DH), lambda b, i: (0, 0)),
                  pl.BlockSpec((1, MEM_DH), lambda b, i: (0, 0))],
        out_specs=pl.BlockSpec((ts, qw), lambda b, i: (row(b, i), 0)),
        compiler_params=_cparams(("arbitrary", "arbitrary")),
        name="memory_attention",
    )(proj, proj, proj, kv, mem_q_g.reshape(1, MEM_DH), mem_k_g.reshape(1, MEM_DH))


MERGE_TN = 256


def _merge_kernel(yr_ref, ya_ref, ym_ref, wr_ref, wa_ref, wm_ref,
                  g0_ref, e0_ref, g1_ref, e1_ref, g2_ref, e2_ref, o_ref):
    acc = None
    for y_ref, w_ref, g_ref, e_ref in ((yr_ref, wr_ref, g0_ref, e0_ref),
                                       (ya_ref, wa_ref, g1_ref, e1_ref),
                                       (ym_ref, wm_ref, g2_ref, e2_ref)):
        g = jnp.concatenate([g_ref[...], e_ref[...]], axis=1)[:, GATE_SHIFT:GATE_SHIFT + MERGE_TN]
        y = _dot(y_ref[...], w_ref[...].astype(BF16))
        term = y / (1.0 + jnp.exp(-g))
        acc = term if acc is None else acc + term
    o_ref[...] = acc.astype(o_ref.dtype)


def _merge(proj, y_ret, y_att, y_mem, w_up_ret, w_up_att, w_up_mem, d_model, tm):
    m = y_ret.shape[0]
    tm = min(tm, m)
    tn = MERGE_TN
    yspec = lambda y: pl.BlockSpec((tm, y.shape[1]), lambda i, n: (i, 0), pipeline_mode=pl.Buffered(1))
    wspec = lambda w: pl.BlockSpec((w.shape[0], tn), lambda i, n: (0, n))
    gspecs = []
    for br in range(3):
        base = GATE_BASE + br * d_model
        gspecs.append(pl.BlockSpec((tm, tn), lambda i, n, base=base: (i, base // tn + n)))
        gspecs.append(pl.BlockSpec((tm, LANE), lambda i, n, base=base: (i, base // LANE + (n + 1) * (tn // LANE))))
    return pl.pallas_call(
        _merge_kernel,
        out_shape=jax.ShapeDtypeStruct((m, d_model), BF16),
        grid=(m // tm, d_model // tn),
        in_specs=[yspec(y_ret), yspec(y_att), yspec(y_mem),
                  wspec(w_up_ret), wspec(w_up_att), wspec(w_up_mem)] + gspecs,
        out_specs=pl.BlockSpec((tm, tn), lambda i, n: (i, n)),
        compiler_params=_cparams(("arbitrary", "arbitrary")),
        name="gated_merge",
    )(y_ret, y_att, y_mem, w_up_ret, w_up_att, w_up_mem, *([proj] * 6))


RANK_NONE = 64.0


def _top_rows(vals, k, with_rank=False):
    rows = []
    rank = jnp.full(vals.shape, RANK_NONE, F32) if with_rank else None
    for r in range(k):
        mx = jnp.max(vals, axis=0, keepdims=True)
        rows.append(mx)
        top = vals == mx
        if with_rank:
            rank = jnp.where(top, r + 1.0, rank)
        vals = jnp.where(top, -jnp.inf, vals)
    return (rows, rank) if with_rank else rows


def _router_kernel(q_ref, keys_ref, u_ref, v_ref, c0_ref, e0_ref, r1_ref, e1_ref, ub_ref, vb_ref):
    ub_ref[...] = u_ref[...].astype(ub_ref.dtype)
    vb_ref[...] = v_ref[...].astype(vb_ref.dtype)
    kk = PEER_TOPK
    s_t = []
    for c in range(2):
        qs = q_ref[:, c * LANE:(c + 1) * LANE].astype(BF16)
        s_t.append(_dot_nt(keys_ref[c].astype(BF16), qs))
    a = _top_rows(s_t[0], kk)
    b, rank1 = _top_rows(s_t[1], kk, with_rank=True)
    amat = jnp.concatenate(a, axis=0)
    bmat = jnp.concatenate(b, axis=0)
    half = kk // 2
    cand = jnp.concatenate([a[0] + bmat] + [a[r] + bmat[:half] for r in range(1, half)]
                           + [amat[half:] + b[0]], axis=0)
    tv = _top_rows(cand, kk)
    tau = tv[kk - 1]
    z = None
    for r in range(kk):
        e = jnp.exp(tv[r] - tv[0])
        z = e if z is None else z + e
    cnt = jnp.zeros(amat.shape, F32)
    for j in range(kk):
        cnt = cnt + jnp.where(amat + b[j] >= tau, 1.0, 0.0)
    count0 = jnp.zeros(s_t[0].shape, F32)
    for r in range(kk):
        count0 = jnp.where(s_t[0] == a[r], cnt[r:r + 1], count0)
    c0_ref[0] = count0
    e0_ref[0] = jnp.exp(s_t[0] - a[0]) * (1.0 / z)
    r1_ref[0] = rank1
    e1_ref[0] = jnp.exp(s_t[1] - b[0])


def _peer_router(q, sub_keys, u_tab, v_tab, tt):
    m = q.shape[0]
    tt = min(tt, m)
    nk = PEER_NKEYS
    ne, d = u_tab.shape
    steps = (m // tt) * PEER_HEADS
    slab = ne // steps
    assert slab * steps == ne and slab % 16 == 0
    keys = sub_keys.reshape(PEER_HEADS * 2, nk, PEER_DQ // 2)
    f_shape = jax.ShapeDtypeStruct((PEER_HEADS, nk, m), F32)
    t_shape = jax.ShapeDtypeStruct((ne, d), BF16)
    s_spec = pl.BlockSpec((1, nk, tt), lambda i, h: (h, 0, i))
    t_spec = pl.BlockSpec((slab, d), lambda i, h: (i * PEER_HEADS + h, 0))
    return pl.pallas_call(
        _router_kernel,
        out_shape=(f_shape,) * 4 + (t_shape,) * 2,
        grid=(m // tt, PEER_HEADS),
        in_specs=[pl.BlockSpec((tt, PEER_DQ), lambda i, h: (i, h)),
                  pl.BlockSpec((2, nk, PEER_DQ // 2), lambda i, h: (h, 0, 0)),
                  t_spec, t_spec],
        out_specs=(s_spec,) * 4 + (t_spec,) * 2,
        compiler_params=_cparams(("arbitrary", "arbitrary")),
        name="peer_router",
    )(q, keys, u_tab, v_tab)


def _gelu(x):
    return 0.5 * x * (1.0 + lax.erf(x * (2.0 ** -0.5)))


def _peer_gate_kernel(h_ref, u_ref, c0_ref, e0_ref, r1_ref, e1_ref, o_ref, z_ref, *, te):
    e = pl.program_id(1)
    nsub = te // PEER_NKEYS
    tm = h_ref.shape[0]
    z_ref[...] = _dot_nt(u_ref[...], h_ref[...])
    for ii in range(nsub):
        i_idx = e * nsub + ii
        rows = slice(ii * PEER_NKEYS, (ii + 1) * PEER_NKEYS)
        c0_rows = [c0_ref[h, pl.ds(i_idx, 1), :] for h in range(PEER_HEADS)]
        e0_rows = [e0_ref[h, pl.ds(i_idx, 1), :] for h in range(PEER_HEADS)]
        for tc in range(tm // LANE):
            cols = slice(tc * LANE, (tc + 1) * LANE)
            w = None
            for h in range(PEER_HEADS):
                hit = r1_ref[h, :, cols] <= c0_rows[h][:, cols]
                term = jnp.where(hit, e1_ref[h, :, cols] * e0_rows[h][:, cols], 0.0)
                w = term if w is None else w + term
            p_t = _gelu(z_ref[rows, cols]) * w
            o_ref[cols, rows] = p_t.T.astype(o_ref.dtype)


def _peer_gate(h2, u_bf, c0t, e0t, r1t, e1t, tm, te):
    m, d = h2.shape
    ne = u_bf.shape[0]
    tm = min(tm, m)
    sspec = pl.BlockSpec((PEER_HEADS, PEER_NKEYS, tm), lambda i, e: (0, 0, i))
    return pl.pallas_call(
        functools.partial(_peer_gate_kernel, te=te),
        out_shape=jax.ShapeDtypeStruct((m, ne), BF16),
        grid=(m // tm, ne // te),
        in_specs=[pl.BlockSpec((tm, d), lambda i, e: (i, 0)),
                  pl.BlockSpec((te, d), lambda i, e: (e, 0)),
                  sspec, sspec, sspec, sspec],
        out_specs=pl.BlockSpec((tm, te), lambda i, e: (i, e)),
        scratch_shapes=[pltpu.VMEM((te, tm), F32)],
        compiler_params=_cparams(("arbitrary", "arbitrary")),
        name="peer_gate",
    )(h2, u_bf, c0t, e0t, r1t, e1t)


def _peer_out_kernel(p_ref, v_ref, x_ref, o_ref):
    @pl.when(pl.program_id(2) == 0)
    def _():
        o_ref[...] = x_ref[...]

    o_ref[...] += _dot(p_ref[...], v_ref[...])


def _peer_out(p, v_bf, x, tm, tn, tk):
    m, ne = p.shape
    d = v_bf.shape[1]
    tm, tn, tk = min(tm, m), min(tn, d), min(tk, ne)
    return pl.pallas_call(
        _peer_out_kernel,
        out_shape=jax.ShapeDtypeStruct((m, d), F32),
        grid=(m // tm, d // tn, ne // tk),
        in_specs=[pl.BlockSpec((tm, tk), lambda i, n, k: (i, k)),
                  pl.BlockSpec((tk, tn), lambda i, n, k: (k, n)),
                  pl.BlockSpec((tm, tn), lambda i, n, k: (i, n))],
        out_specs=pl.BlockSpec((tm, tn), lambda i, n, k: (i, n)),
        compiler_params=_cparams(("arbitrary", "arbitrary", "arbitrary")),
        name="peer_out",
    )(p, v_bf, x)


def kernel(x, mem, positions, ln1_g, w_in, att_q_g, att_k_g, rel_bias, mem_g, w_mem_kv, mem_q_g, mem_k_g,
           w_up_ret, w_up_att, w_up_mem, w_out, ln2_g, peer_w_q, peer_sub_keys, peer_u, peer_v):
    bsz, seq, d = x.shape
    m = bsz * seq
    depth = ln1_g.shape[0]
    xf = x.reshape(m, d)
    memf = mem.reshape(bsz * mem.shape[1], d)
    t = Tiles
    for l in range(depth):
        h = _rmsnorm(xf, ln1_g[l], t.NORM_ROWS)
        proj = _matmul_ring(h, jnp.swapaxes(w_in[l], 0, 1), tm=t.IN_PROJ[0], tn=t.IN_PROJ[1],
                            w_transposed=True, name="in_proj")
        y_ret = _retention(proj, positions, bsz, seq)
        kn, vt, ikb = _kprep(proj, att_k_g[l], t.KPREP_ROWS)
        y_att = _sparse_attention(proj, kn, vt, ikb, att_q_g[l], rel_bias, bsz, seq)
        kv = _matmul(_rmsnorm(memf, mem_g[l], t.NORM_ROWS), w_mem_kv[l],
                     tm=t.MEM_KV[0], tn=t.MEM_KV[1], name="mem_kv")
        y_mem = _memory_attention(proj, kv, mem_q_g[l], mem_k_g[l], bsz, seq, t.MEM_ATT_ROWS)
        merged = _merge(proj, y_ret, y_att, y_mem, w_up_ret[l], w_up_att[l], w_up_mem[l], d, t.MERGE_ROWS)
        xf = _matmul_ring(merged, w_out[l], tm=t.OUT_PROJ[0], tn=t.OUT_PROJ[1], residual=xf, name="out_proj")
        h2 = _rmsnorm(xf, ln2_g[l], t.NORM_ROWS)
        q = _matmul(h2, peer_w_q[l], tm=t.PEER_QUERY[0], tn=t.PEER_QUERY[1], name="peer_query")
        c0t, e0t, r1t, e1t, u_bf, v_bf = _peer_router(q, peer_sub_keys[l], peer_u[l], peer_v[l],
                                                      t.ROUTER_TOKENS)
        p = _peer_gate(h2, u_bf, c0t, e0t, r1t, e1t, *t.GATE)
        xf = _peer_out(p, v_bf, xf, *t.PEER_OUT)
    return xf.reshape(bsz, seq, d)
```

```python
import functools
import math

import jax
import jax.numpy as jnp
from jax import lax
from jax.experimental import pallas as pl
from jax.experimental.pallas import tpu as pltpu

F32 = jnp.float32
BF16 = jnp.bfloat16
I32 = jnp.int32

MEM_LEN = 256
RET_HEADS, RET_DK, RET_DV, RET_CHUNK = 8, 128, 256, 128
ATT_HEADS, ATT_DH = 8, 128
IDX_HEADS, IDX_DH = 16, 64
TOPK_MAX = 256
MEM_HEADS, MEM_DH = 4, 256
REL_BUCKETS, REL_MAX_DIST = 32, 128
PEER_HEADS, PEER_NKEYS, PEER_DQ, PEER_TOPK = 8, 128, 256, 16
EPS = 1e-6
ROPE_BASE = 10000.0

OFF_RQ = 0
OFF_RK = OFF_RQ + RET_HEADS * RET_DK
OFF_RV = OFF_RK + RET_HEADS * RET_DK
OFF_RG = OFF_RV + RET_HEADS * RET_DV
OFF_AQ = OFF_RG + RET_HEADS * RET_DV
OFF_AK = OFF_AQ + ATT_HEADS * ATT_DH
OFF_AV = OFF_AK + ATT_DH
OFF_IQ = OFF_AV + ATT_DH
OFF_IK = OFF_IQ + IDX_HEADS * IDX_DH
OFF_IW = OFF_IK + IDX_DH
OFF_MQ = OFF_IW + IDX_HEADS
OFF_GATE = OFF_MQ + MEM_HEADS * MEM_DH

LANE = 128
GATE_SHIFT = OFF_GATE % LANE
GATE_BASE = OFF_GATE - GATE_SHIFT
MQ_SHIFT = OFF_MQ - OFF_IK
VMEM_LIMIT = 56 * 1024 * 1024


class Tiles:
    NORM_ROWS = 256
    IN_PROJ = (1024, 512)
    OUT_PROJ = (1024, 512)
    PEER_QUERY = (1024, 512)
    MEM_KV = (512, 512)
    KPREP_ROWS = 512
    MEM_ATT_ROWS = 512
    MERGE_ROWS = 2048
    ROUTER_TOKENS = 1024
    GATE = (512, 1024)
    PEER_OUT = (512, 2048, 2048)

INT_MIN = -(2 ** 31)
NEG_BIG = -1e30


def _cparams(sem):
    return pltpu.CompilerParams(dimension_semantics=sem, vmem_limit_bytes=VMEM_LIMIT)


def _dot_nt(a, b):
    return lax.dot_general(a, b, (((1,), (1,)), ((), ())), preferred_element_type=F32)


def _dot(a, b):
    return jnp.dot(a, b, preferred_element_type=F32)


def _rms(x):
    return x * lax.rsqrt(jnp.mean(x * x, axis=-1, keepdims=True) + EPS)


def _rmsnorm_kernel(x_ref, g_ref, o_ref):
    o_ref[...] = (_rms(x_ref[...].astype(F32)) * g_ref[...]).astype(o_ref.dtype)


def _rmsnorm(x, g, tm):
    m, d = x.shape
    tm = min(tm, m)
    return pl.pallas_call(
        _rmsnorm_kernel,
        out_shape=jax.ShapeDtypeStruct((m, d), BF16),
        grid=(m // tm,),
        in_specs=[pl.BlockSpec((tm, d), lambda i: (i, 0)),
                  pl.BlockSpec((1, d), lambda i: (0, 0))],
        out_specs=pl.BlockSpec((tm, d), lambda i: (i, 0)),
        compiler_params=_cparams(("arbitrary",)),
        name="rmsnorm",
    )(x, g.reshape(1, d))


def _mm_kernel(a_ref, w_ref, o_ref, wsc):
    @pl.when(pl.program_id(1) == 0)
    def _():
        wsc[...] = w_ref[...].astype(BF16)

    o_ref[...] = _dot(a_ref[...], wsc[...]).astype(o_ref.dtype)


def _matmul(a, w, *, tm, tn, name):
    m, k = a.shape
    n = w.shape[1]
    tm = min(tm, m)
    tn = min(tn, n)
    return pl.pallas_call(
        _mm_kernel,
        out_shape=jax.ShapeDtypeStruct((m, n), F32),
        grid=(pl.cdiv(n, tn), m // tm),
        in_specs=[pl.BlockSpec((tm, k), lambda j, i: (i, 0)),
                  pl.BlockSpec((k, tn), lambda j, i: (0, j))],
        out_specs=pl.BlockSpec((tm, tn), lambda j, i: (i, j)),
        scratch_shapes=[pltpu.VMEM((k, tn), BF16)],
        compiler_params=_cparams(("arbitrary", "arbitrary")),
        name=name,
    )(a, w)


A_RING = 3


def _mm_ring_kernel(*refs, has_res, w_transposed, tm, ni, nsteps):
    if has_res:
        a_hbm, w_ref, r_ref, o_ref, abuf, sem, wsc = refs
    else:
        a_hbm, w_ref, o_ref, abuf, sem, wsc = refs
    i = pl.program_id(1)
    s = pl.program_id(0) * ni + i

    def a_copy(step, slot):
        row = pl.multiple_of(lax.rem(step, ni) * tm, tm)
        return pltpu.make_async_copy(a_hbm.at[pl.ds(row, tm), :], abuf.at[slot], sem.at[slot])

    @pl.when(s == 0)
    def _():
        a_copy(0, 0).start()
        a_copy(1, 1).start()

    @pl.when(s + 2 < nsteps)
    def _():
        a_copy(s + 2, lax.rem(s + 2, A_RING)).start()

    @pl.when(i == 0)
    def _():
        wsc[...] = w_ref[...].astype(BF16)

    slot = lax.rem(s, A_RING)
    a_copy(s, slot).wait()
    acc = _dot_nt(abuf[slot], wsc[...]) if w_transposed else _dot(abuf[slot], wsc[...])
    if has_res:
        acc = acc + r_ref[...]
    o_ref[...] = acc.astype(o_ref.dtype)


def _matmul_ring(a, w, *, tm, tn, residual=None, out_dtype=F32, w_transposed=False, name="matmul_ring"):
    m, k = a.shape
    n = w.shape[0] if w_transposed else w.shape[1]
    tm = min(tm, m)
    tn = min(tn, n)
    ni = m // tm
    nj = pl.cdiv(n, tn)
    assert ni * nj >= 2
    w_block = (tn, k) if w_transposed else (k, tn)
    w_index = (lambda j, i: (j, 0)) if w_transposed else (lambda j, i: (0, j))
    in_specs = [pl.BlockSpec(memory_space=pl.ANY), pl.BlockSpec(w_block, w_index)]
    args = [a, w]
    if residual is not None:
        in_specs.append(pl.BlockSpec((tm, tn), lambda j, i: (i, j)))
        args.append(residual)
    return pl.pallas_call(
        functools.partial(_mm_ring_kernel, has_res=residual is not None, w_transposed=w_transposed,
                          tm=tm, ni=ni, nsteps=ni * nj),
        out_shape=jax.ShapeDtypeStruct((m, n), out_dtype),
        grid=(nj, ni),
        in_specs=in_specs,
        out_specs=pl.BlockSpec((tm, tn), lambda j, i: (i, j)),
        scratch_shapes=[pltpu.VMEM((A_RING, tm, k), BF16),
                        pltpu.SemaphoreType.DMA((A_RING,)),
                        pltpu.VMEM(w_block, BF16)],
        compiler_params=_cparams(("arbitrary", "arbitrary")),
        name=name,
    )(*args)


def _ret_kernel(cdec_ref, rq_ref, rk_ref, rv_ref, rg_ref, pos_ref, invf_ref, sign_ref,
                idec_ref, qdec_ref, kdec_ref, o_ref, st_ref):
    @pl.when(pl.program_id(1) == 0)
    def _():
        st_ref[...] = jnp.zeros_like(st_ref)

    ang = pos_ref[...].astype(F32) * invf_ref[...]
    cos = jnp.cos(ang)
    sin = jnp.sin(ang) * sign_ref[...]
    scale = RET_DK ** -0.5
    for h in range(RET_HEADS):
        q = rq_ref[:, h * RET_DK:(h + 1) * RET_DK]
        k = rk_ref[:, h * RET_DK:(h + 1) * RET_DK]
        q = q * cos + pltpu.roll(q, RET_DK // 2, 1) * sin
        k = (k * cos + pltpu.roll(k, RET_DK // 2, 1) * sin) * scale
        vb = rv_ref[:, h * RET_DV:(h + 1) * RET_DV].astype(BF16)
        s = _dot_nt(q.astype(BF16), k.astype(BF16)) * idec_ref[h]
        inner = _dot(s.astype(BF16), vb)
        st = st_ref[h]
        cross = _dot((q * qdec_ref[h]).astype(BF16), st.astype(BF16))
        kd = (k * kdec_ref[h]).astype(BF16)
        upd = lax.dot_general(kd, vb, (((0,), (0,)), ((), ())), preferred_element_type=F32)
        st_ref[h] = st * cdec_ref[h] + upd
        y = _rms(inner + cross)
        g = rg_ref[:, h * RET_DV:(h + 1) * RET_DV]
        o_ref[:, h * RET_DV:(h + 1) * RET_DV] = (g / (1.0 + jnp.exp(-g)) * y).astype(o_ref.dtype)


def _retention(proj, positions, bsz, seq):
    c = RET_CHUNK
    nc = seq // c
    m = bsz * seq
    hh = RET_HEADS
    log_gamma = jnp.log1p(-jnp.exp2(-5.0 - jnp.arange(hh, dtype=F32)))
    idx = jnp.arange(c, dtype=F32)
    diff = idx[:, None] - idx[None, :]
    idec = jnp.where(diff[None] >= 0,
                     jnp.exp(log_gamma[:, None, None] * jnp.maximum(diff, 0.0)[None]), 0.0)
    qdec = jnp.exp(log_gamma[:, None] * (idx + 1.0)[None])
    kdec = jnp.exp(log_gamma[:, None] * (c - 1.0 - idx)[None])
    cdec = jnp.exp(log_gamma * c)
    qdec_b = jnp.broadcast_to(qdec[:, :, None], (hh, c, RET_DK))
    kdec_b = jnp.broadcast_to(kdec[:, :, None], (hh, c, RET_DK))
    half = RET_DK // 2
    inv_freq = 1.0 / (ROPE_BASE ** (jnp.arange(half, dtype=F32) / half))
    invf = jnp.concatenate([inv_freq, inv_freq]).reshape(1, RET_DK)
    sign = jnp.concatenate([-jnp.ones((half,), F32), jnp.ones((half,), F32)]).reshape(1, RET_DK)
    qk_w = hh * RET_DK
    v_w = hh * RET_DV
    row = lambda b, i: b * nc + i
    const3 = lambda b, i: (0, 0, 0)
    return pl.pallas_call(
        _ret_kernel,
        out_shape=jax.ShapeDtypeStruct((m, v_w), BF16),
        grid=(bsz, nc),
        in_specs=[
            pl.BlockSpec(memory_space=pltpu.SMEM),
            pl.BlockSpec((c, qk_w), lambda b, i: (row(b, i), OFF_RQ // qk_w)),
            pl.BlockSpec((c, qk_w), lambda b, i: (row(b, i), OFF_RK // qk_w)),
            pl.BlockSpec((c, v_w), lambda b, i: (row(b, i), OFF_RV // v_w)),
            pl.BlockSpec((c, v_w), lambda b, i: (row(b, i), OFF_RG // v_w)),
            pl.BlockSpec((c, 1), lambda b, i: (row(b, i), 0)),
            pl.BlockSpec((1, RET_DK), lambda b, i: (0, 0)),
            pl.BlockSpec((1, RET_DK), lambda b, i: (0, 0)),
            pl.BlockSpec((hh, c, c), const3),
            pl.BlockSpec((hh, c, RET_DK), const3),
            pl.BlockSpec((hh, c, RET_DK), const3),
        ],
        out_specs=pl.BlockSpec((c, v_w), lambda b, i: (row(b, i), 0)),
        scratch_shapes=[pltpu.VMEM((hh, RET_DK, RET_DV), F32)],
        compiler_params=_cparams(("arbitrary", "arbitrary")),
        name="retention",
    )(cdec, proj, proj, proj, proj, positions.reshape(m, 1), invf, sign, idec, qdec_b, kdec_b)


def _kprep_kernel(ak_ref, av_ref, ikw_ref, kg_ref, kn_ref, vt_ref, ik_ref):
    kn_ref[...] = (_rms(ak_ref[...]) * kg_ref[...]).astype(BF16)
    vt_ref[...] = av_ref[...].T.astype(BF16)
    ik_ref[...] = ikw_ref[...].astype(BF16)


def _kprep(proj, att_k_g, tm):
    m = proj.shape[0]
    tm = min(tm, m)
    blk = lambda off: pl.BlockSpec((tm, LANE), lambda i: (i, off // LANE))
    nat = jax.ShapeDtypeStruct((m, LANE), BF16)
    nat_spec = pl.BlockSpec((tm, LANE), lambda i: (i, 0))
    return pl.pallas_call(
        _kprep_kernel,
        out_shape=(nat, jax.ShapeDtypeStruct((LANE, m), BF16), nat),
        grid=(m // tm,),
        in_specs=[blk(OFF_AK), blk(OFF_AV), blk(OFF_IK),
                  pl.BlockSpec((1, LANE), lambda i: (0, 0))],
        out_specs=(nat_spec, pl.BlockSpec((LANE, tm), lambda i: (0, i)), nat_spec),
        compiler_params=_cparams(("arbitrary",)),
        name="att_kprep",
    )(proj, proj, proj, att_k_g.reshape(1, ATT_DH))


IDX_TK = 512
FAR_TK = 512
M_INIT = -1e29


def _att_kernel(aq_ref, iq0_ref, iq1_ref, iq2_ref, iq3_ref, ikw_ref, kn_ref, vt_ref, ik_ref,
                qg_ref, bias_ref, o_ref, sc_ref, qt_ref, acc_ref, *, topk, tq, tki):
    qi = pl.program_id(1)
    t0 = qi * tq
    nkb = (t0 + tq + tki - 1) // tki
    col_t = t0 + lax.broadcasted_iota(I32, (1, tq), 1)
    sub = tki // LANE

    heads_t = []
    for ref in (iq0_ref, iq1_ref, iq2_ref, iq3_ref):
        for c in range(2):
            pair_t = ref[:, c * LANE:(c + 1) * LANE].T
            heads_t.append(pair_t[:IDX_DH])
            heads_t.append(pair_t[IDX_DH:])
    rhs_all = jnp.concatenate(heads_t, axis=1).astype(BF16)
    w_t = ikw_ref[...].T
    idx_scale = (IDX_DH ** -0.5) * (IDX_HEADS ** -0.5)

    def score_blk(kb, carry):
        k0 = pl.multiple_of(kb * tki, tki)
        ikb = ik_ref[pl.ds(k0, tki), :][:, :IDX_DH]
        lg = _dot(ikb, rhs_all)
        acc = None
        for h in range(IDX_HEADS):
            term = w_t[IDX_DH + h:IDX_DH + h + 1, :] * jnp.maximum(lg[:, h * tq:(h + 1) * tq], 0.0)
            acc = term if acc is None else acc + term
        acc = acc * idx_scale
        s_idx = k0 + lax.broadcasted_iota(I32, (tki, 1), 0)
        acc = jnp.where(s_idx <= col_t, acc, -jnp.inf)
        bits = pltpu.bitcast(acc, I32)
        sc_ref[pl.ds(k0, tki), :] = jnp.where(bits < 0, bits ^ 0x7FFFFFFF, bits)
        return carry

    lax.fori_loop(0, nkb, score_blk, 0)

    def count_ge(cand):
        def body(kb, cnt):
            k0 = pl.multiple_of(kb * tki, tki)
            c = jnp.where(sc_ref[pl.ds(k0, tki), :] >= cand, 1, 0)
            return cnt + jnp.sum(c.reshape(tki // 8, 8, tq), axis=0)
        cnt = lax.fori_loop(0, nkb, body, jnp.zeros((8, tq), I32))
        return jnp.sum(cnt, axis=0, keepdims=True)

    base = jnp.where(count_ge(jnp.zeros((1, tq), I32)) >= topk, 0, INT_MIN).astype(I32)

    def bit_body(i, base):
        cand = base | jnp.left_shift(jnp.int32(1), 30 - i)
        return jnp.where(count_ge(cand) >= topk, cand, base)

    tau = lax.fori_loop(0, 31, bit_body, base)

    qg = qg_ref[...]
    scale = ATT_DH ** -0.5
    tkf = min(FAR_TK, tki)
    fsub = tkf // LANE
    nfar = jnp.maximum(qi - 1, 0) // fsub

    for h in range(ATT_HEADS):
        qt_ref[h] = (_rms(aq_ref[:, h * ATT_DH:(h + 1) * ATT_DH]) * qg).T.astype(BF16)
    acc_ref[...] = jnp.zeros_like(acc_ref)

    def softmax_block(carry, k0, width, sel, bias_of=None, shift_of=None):
        m_all, l_all = carry
        kblk = kn_ref[pl.ds(k0, width), :]
        vblk = vt_ref[:, pl.ds(k0, width)]
        m_rows, l_rows = [], []
        for h in range(ATT_HEADS):
            lg = _dot(kblk, qt_ref[h]) * scale
            if bias_of is not None:
                lg = lg + bias_of(h)
            lg = jnp.where(sel, lg, NEG_BIG)
            m_i = m_all[h:h + 1]
            m_blk = jnp.max(lg, axis=0, keepdims=True)
            if shift_of is not None:
                m_blk = m_blk + shift_of(h)
            m_new = jnp.maximum(m_i, m_blk)
            alpha = jnp.exp(m_i - m_new)
            p = jnp.exp(lg - (m_new if shift_of is None else m_new - shift_of(h)))
            m_rows.append(m_new)
            l_rows.append(alpha * l_all[h:h + 1] + jnp.sum(p, axis=0, keepdims=True))
            acc_ref[h] = alpha * acc_ref[h] + _dot(vblk, p.astype(BF16))
        return jnp.concatenate(m_rows, axis=0), jnp.concatenate(l_rows, axis=0)

    def far_blk(kb, carry):
        k0 = pl.multiple_of(kb * tkf, tkf)
        sel = sc_ref[pl.ds(k0, tkf), :] >= tau
        return softmax_block(carry, k0, tkf, sel, shift_of=lambda h: bias_ref[2, h, 0:1, :])

    def near_blk(j, carry):
        k0 = pl.multiple_of(j * LANE, LANE)
        s_idx = k0 + lax.broadcasted_iota(I32, (LANE, 1), 0)
        sel = jnp.logical_and(sc_ref[pl.ds(k0, LANE), :] >= tau, s_idx <= col_t)
        d = jnp.minimum(qi - j, 2)
        return softmax_block(carry, k0, LANE, sel, bias_of=lambda h: bias_ref[d, h])

    init = (jnp.full((ATT_HEADS, tq), M_INIT, F32), jnp.zeros((ATT_HEADS, tq), F32))
    carry = lax.fori_loop(0, nfar, far_blk, init)
    _, l_f = lax.fori_loop(nfar * fsub, qi + 1, near_blk, carry)
    for h in range(ATT_HEADS):
        o_ref[:, h * ATT_DH:(h + 1) * ATT_DH] = (acc_ref[h] / l_f[h:h + 1]).T.astype(o_ref.dtype)


def _t5_bucket(dist):
    n = jnp.maximum(dist, 0)
    max_exact = REL_BUCKETS // 2
    large = max_exact + (jnp.log(jnp.maximum(n, 1).astype(F32) / max_exact)
                         / math.log(REL_MAX_DIST / max_exact) * (REL_BUCKETS - max_exact)).astype(I32)
    large = jnp.minimum(large, REL_BUCKETS - 1)
    return jnp.where(n < max_exact, n, large)


def _sparse_attention(proj, kn, vt, ikb, att_q_g, rel_bias, bsz, seq):
    tq = LANE
    m = bsz * seq
    nq = seq // tq
    topk = min(TOPK_MAX, seq // 4)
    tki = min(IDX_TK, seq)
    kk = jnp.arange(LANE, dtype=I32)[:, None]
    qq = jnp.arange(tq, dtype=I32)[None, :]
    dist = jnp.stack([d * LANE + qq - kk for d in range(3)])
    onehot = (_t5_bucket(dist)[..., None] == jnp.arange(REL_BUCKETS, dtype=I32)).astype(F32)
    bias = jnp.einsum("dkqb,bh->dhkq", onehot, rel_bias.astype(F32), precision=lax.Precision.HIGHEST)
    qw = ATT_HEADS * ATT_DH
    row = lambda b, i: b * nq + i
    iq_spec = lambda p: pl.BlockSpec((tq, 2 * LANE), lambda b, i: (row(b, i), OFF_IQ // (2 * LANE) + p))
    kspec = pl.BlockSpec((seq, LANE), lambda b, i: (b, 0))
    return pl.pallas_call(
        functools.partial(_att_kernel, topk=topk, tq=tq, tki=tki),
        out_shape=jax.ShapeDtypeStruct((m, qw), BF16),
        grid=(bsz, nq),
        in_specs=[
            pl.BlockSpec((tq, qw), lambda b, i: (row(b, i), OFF_AQ // qw)),
            iq_spec(0), iq_spec(1), iq_spec(2), iq_spec(3),
            pl.BlockSpec((tq, LANE), lambda b, i: (row(b, i), OFF_IK // LANE)),
            kspec,
            pl.BlockSpec((LANE, seq), lambda b, i: (0, b)),
            kspec,
            pl.BlockSpec((1, ATT_DH), lambda b, i: (0, 0)),
            pl.BlockSpec((3, ATT_HEADS, LANE, tq), lambda b, i: (0, 0, 0, 0)),
        ],
        out_specs=pl.BlockSpec((tq, qw), lambda b, i: (row(b, i), 0)),
        scratch_shapes=[pltpu.VMEM((seq, tq), I32),
                        pltpu.VMEM((ATT_HEADS, ATT_DH, tq), BF16),
                        pltpu.VMEM((ATT_HEADS, ATT_DH, tq), F32)],
        compiler_params=_cparams(("arbitrary", "arbitrary")),
        name="sparse_attention",
    )(proj, proj, proj, proj, proj, proj, kn, vt, ikb, att_q_g.reshape(1, ATT_DH), bias)


MQ_WIN = 3 * LANE


def _mem_kernel(w0_ref, w1_ref, w2_ref, kv_ref, qg_ref, kg_ref, o_ref):
    win = jnp.concatenate([w0_ref[...], w1_ref[...], w2_ref[...]], axis=1)
    mq = win[:, MQ_SHIFT:MQ_SHIFT + MEM_HEADS * MEM_DH]
    qg = qg_ref[...]
    kg = kg_ref[...]
    scale = MEM_DH ** -0.5
    kvw = MEM_HEADS * MEM_DH
    for h in range(MEM_HEADS):
        q = (_rms(mq[:, h * MEM_DH:(h + 1) * MEM_DH]) * qg).astype(BF16)
        k = (_rms(kv_ref[:, h * MEM_DH:(h + 1) * MEM_DH]) * kg).astype(BF16)
        v = kv_ref[:, kvw + h * MEM_DH:kvw + (h + 1) * MEM_DH].astype(BF16)
        lg = _dot_nt(q, k) * scale
        p = jnp.exp(lg - jnp.max(lg, axis=-1, keepdims=True))
        p = p / jnp.sum(p, axis=-1, keepdims=True)
        o_ref[:, h * MEM_DH:(h + 1) * MEM_DH] = _dot(p.astype(BF16), v).astype(o_ref.dtype)


def _memory_attention(proj, kv, mem_q_g, mem_k_g, bsz, seq, ts):
    ts = min(ts, seq)
    ns = seq // ts
    m = bsz * seq
    qw = MEM_HEADS * MEM_DH
    row = lambda b, i: b * ns + i
    wspec = lambda p: pl.BlockSpec((ts, MQ_WIN), lambda b, i: (row(b, i), OFF_IK // MQ_WIN + p))
    return pl.pallas_call(
        _mem_kernel,
        out_shape=jax.ShapeDtypeStruct((m, qw), BF16),
        grid=(bsz, ns),
        in_specs=[wspec(0), wspec(1), wspec(2),
                  pl.BlockSpec((MEM_LEN, 2 * qw), lambda b, i: (b, 0)),
                  pl.BlockSpec((1, MEM_DH), lambda b, i: (0, 0)),
                  pl.BlockSpec((1, MEM_DH), lambda b, i: (0, 0))],
        out_specs=pl.BlockSpec((ts, qw), lambda b, i: (row(b, i), 0)),
        compiler_params=_cparams(("arbitrary", "arbitrary")),
        name="memory_attention",
    )(proj, proj, proj, kv, mem_q_g.reshape(1, MEM_DH), mem_k_g.reshape(1, MEM_DH))


MERGE_TN = 256


def _merge_kernel(yr_ref, ya_ref, ym_ref, wr_ref, wa_ref, wm_ref,
                  g0_ref, e0_ref, g1_ref, e1_ref, g2_ref, e2_ref, o_ref):
    acc = None
    for y_ref, w_ref, g_ref, e_ref in ((yr_ref, wr_ref, g0_ref, e0_ref),
                                       (ya_ref, wa_ref, g1_ref, e1_ref),
                                       (ym_ref, wm_ref, g2_ref, e2_ref)):
        g = jnp.concatenate([g_ref[...], e_ref[...]], axis=1)[:, GATE_SHIFT:GATE_SHIFT + MERGE_TN]
        y = _dot(y_ref[...], w_ref[...].astype(BF16))
        term = y / (1.0 + jnp.exp(-g))
        acc = term if acc is None else acc + term
    o_ref[...] = acc.astype(o_ref.dtype)


def _merge(proj, y_ret, y_att, y_mem, w_up_ret, w_up_att, w_up_mem, d_model, tm):
    m = y_ret.shape[0]
    tm = min(tm, m)
    tn = MERGE_TN
    yspec = lambda y: pl.BlockSpec((tm, y.shape[1]), lambda i, n: (i, 0), pipeline_mode=pl.Buffered(1))
    wspec = lambda w: pl.BlockSpec((w.shape[0], tn), lambda i, n: (0, n))
    gspecs = []
    for br in range(3):
        base = GATE_BASE + br * d_model
        gspecs.append(pl.BlockSpec((tm, tn), lambda i, n, base=base: (i, base // tn + n)))
        gspecs.append(pl.BlockSpec((tm, LANE), lambda i, n, base=base: (i, base // LANE + (n + 1) * (tn // LANE))))
    return pl.pallas_call(
        _merge_kernel,
        out_shape=jax.ShapeDtypeStruct((m, d_model), BF16),
        grid=(m // tm, d_model // tn),
        in_specs=[yspec(y_ret), yspec(y_att), yspec(y_mem),
                  wspec(w_up_ret), wspec(w_up_att), wspec(w_up_mem)] + gspecs,
        out_specs=pl.BlockSpec((tm, tn), lambda i, n: (i, n)),
        compiler_params=_cparams(("arbitrary", "arbitrary")),
        name="gated_merge",
    )(y_ret, y_att, y_mem, w_up_ret, w_up_att, w_up_mem, *([proj] * 6))


RANK_NONE = 64.0


def _top_rows(vals, k, with_rank=False):
    rows = []
    rank = jnp.full(vals.shape, RANK_NONE, F32) if with_rank else None
    for r in range(k):
        mx = jnp.max(vals, axis=0, keepdims=True)
        rows.append(mx)
        top = vals == mx
        if with_rank:
            rank = jnp.where(top, r + 1.0, rank)
        vals = jnp.where(top, -jnp.inf, vals)
    return (rows, rank) if with_rank else rows


def _router_kernel(q_ref, keys_ref, u_ref, v_ref, c0_ref, e0_ref, r1_ref, e1_ref, ub_ref, vb_ref):
    ub_ref[...] = u_ref[...].astype(ub_ref.dtype)
    vb_ref[...] = v_ref[...].astype(vb_ref.dtype)
    kk = PEER_TOPK
    s_t = []
    for c in range(2):
        qs = q_ref[:, c * LANE:(c + 1) * LANE].astype(BF16)
        s_t.append(_dot_nt(keys_ref[c].astype(BF16), qs))
    a = _top_rows(s_t[0], kk)
    b, rank1 = _top_rows(s_t[1], kk, with_rank=True)
    amat = jnp.concatenate(a, axis=0)
    bmat = jnp.concatenate(b, axis=0)
    half = kk // 2
    cand = jnp.concatenate([a[0] + bmat] + [a[r] + bmat[:half] for r in range(1, half)]
                           + [amat[half:] + b[0]], axis=0)
    tv = _top_rows(cand, kk)
    tau = tv[kk - 1]
    z = None
    for r in range(kk):
        e = jnp.exp(tv[r] - tv[0])
        z = e if z is None else z + e
    cnt = jnp.zeros(amat.shape, F32)
    for j in range(kk):
        cnt = cnt + jnp.where(amat + b[j] >= tau, 1.0, 0.0)
    count0 = jnp.zeros(s_t[0].shape, F32)
    for r in range(kk):
        count0 = jnp.where(s_t[0] == a[r], cnt[r:r + 1], count0)
    c0_ref[0] = count0
    e0_ref[0] = jnp.exp(s_t[0] - a[0]) * (1.0 / z)
    r1_ref[0] = rank1
    e1_ref[0] = jnp.exp(s_t[1] - b[0])


def _peer_router(q, sub_keys, u_tab, v_tab, tt):
    m = q.shape[0]
    tt = min(tt, m)
    nk = PEER_NKEYS
    ne, d = u_tab.shape
    steps = (m // tt) * PEER_HEADS
    slab = ne // steps
    assert slab * steps == ne and slab % 16 == 0
    keys = sub_keys.reshape(PEER_HEADS * 2, nk, PEER_DQ // 2)
    f_shape = jax.ShapeDtypeStruct((PEER_HEADS, nk, m), F32)
    t_shape = jax.ShapeDtypeStruct((ne, d), BF16)
    s_spec = pl.BlockSpec((1, nk, tt), lambda i, h: (h, 0, i))
    t_spec = pl.BlockSpec((slab, d), lambda i, h: (i * PEER_HEADS + h, 0))
    return pl.pallas_call(
        _router_kernel,
        out_shape=(f_shape,) * 4 + (t_shape,) * 2,
        grid=(m // tt, PEER_HEADS),
        in_specs=[pl.BlockSpec((tt, PEER_DQ), lambda i, h: (i, h)),
                  pl.BlockSpec((2, nk, PEER_DQ // 2), lambda i, h: (h, 0, 0)),
                  t_spec, t_spec],
        out_specs=(s_spec,) * 4 + (t_spec,) * 2,
        compiler_params=_cparams(("arbitrary", "arbitrary")),
        name="peer_router",
    )(q, keys, u_tab, v_tab)


def _gelu(x):
    return 0.5 * x * (1.0 + lax.erf(x * (2.0 ** -0.5)))


def _peer_gate_kernel(h_ref, u_ref, c0_ref, e0_ref, r1_ref, e1_ref, o_ref, z_ref, *, te):
    e = pl.program_id(1)
    nsub = te // PEER_NKEYS
    tm = h_ref.shape[0]
    z_ref[...] = _dot_nt(u_ref[...], h_ref[...])
    for ii in range(nsub):
        i_idx = e * nsub + ii
        rows = slice(ii * PEER_NKEYS, (ii + 1) * PEER_NKEYS)
        c0_rows = [c0_ref[h, pl.ds(i_idx, 1), :] for h in range(PEER_HEADS)]
        e0_rows = [e0_ref[h, pl.ds(i_idx, 1), :] for h in range(PEER_HEADS)]
        for tc in range(tm // LANE):
            cols = slice(tc * LANE, (tc + 1) * LANE)
            w = None
            for h in range(PEER_HEADS):
                hit = r1_ref[h, :, cols] <= c0_rows[h][:, cols]
                term = jnp.where(hit, e1_ref[h, :, cols] * e0_rows[h][:, cols], 0.0)
                w = term if w is None else w + term
            p_t = _gelu(z_ref[rows, cols]) * w
            o_ref[cols, rows] = p_t.T.astype(o_ref.dtype)


def _peer_gate(h2, u_bf, c0t, e0t, r1t, e1t, tm, te):
    m, d = h2.shape
    ne = u_bf.shape[0]
    tm = min(tm, m)
    sspec = pl.BlockSpec((PEER_HEADS, PEER_NKEYS, tm), lambda i, e: (0, 0, i))
    return pl.pallas_call(
        functools.partial(_peer_gate_kernel, te=te),
        out_shape=jax.ShapeDtypeStruct((m, ne), BF16),
        grid=(m // tm, ne // te),
        in_specs=[pl.BlockSpec((tm, d), lambda i, e: (i, 0)),
                  pl.BlockSpec((te, d), lambda i, e: (e, 0)),
                  sspec, sspec, sspec, sspec],
        out_specs=pl.BlockSpec((tm, te), lambda i, e: (i, e)),
        scratch_shapes=[pltpu.VMEM((te, tm), F32)],
        compiler_params=_cparams(("arbitrary", "arbitrary")),
        name="peer_gate",
    )(h2, u_bf, c0t, e0t, r1t, e1t)


def _peer_out_kernel(p_ref, v_ref, x_ref, o_ref):
    @pl.when(pl.program_id(2) == 0)
    def _():
        o_ref[...] = x_ref[...]

    o_ref[...] += _dot(p_ref[...], v_ref[...])


def _peer_out(p, v_bf, x, tm, tn, tk):
    m, ne = p.shape
    d = v_bf.shape[1]
    tm, tn, tk = min(tm, m), min(tn, d), min(tk, ne)
    return pl.pallas_call(
        _peer_out_kernel,
        out_shape=jax.ShapeDtypeStruct((m, d), F32),
        grid=(m // tm, d // tn, ne // tk),
        in_specs=[pl.BlockSpec((tm, tk), lambda i, n, k: (i, k)),
                  pl.BlockSpec((tk, tn), lambda i, n, k: (k, n)),
                  pl.BlockSpec((tm, tn), lambda i, n, k: (i, n))],
        out_specs=pl.BlockSpec((tm, tn), lambda i, n, k: (i, n)),
        compiler_params=_cparams(("arbitrary", "arbitrary", "arbitrary")),
        name="peer_out",
    )(p, v_bf, x)


def kernel(x, mem, positions, ln1_g, w_in, att_q_g, att_k_g, rel_bias, mem_g, w_mem_kv, mem_q_g, mem_k_g,
           w_up_ret, w_up_att, w_up_mem, w_out, ln2_g, peer_w_q, peer_sub_keys, peer_u, peer_v):
    bsz, seq, d = x.shape
    m = bsz * seq
    depth = ln1_g.shape[0]
    xf = x.reshape(m, d)
    memf = mem.reshape(bsz * mem.shape[1], d)
    t = Tiles
    for l in range(depth):
        h = _rmsnorm(xf, ln1_g[l], t.NORM_ROWS)
        proj = _matmul_ring(h, jnp.swapaxes(w_in[l], 0, 1), tm=t.IN_PROJ[0], tn=t.IN_PROJ[1],
                            w_transposed=True, name="in_proj")
        y_ret = _retention(proj, positions, bsz, seq)
        kn, vt, ikb = _kprep(proj, att_k_g[l], t.KPREP_ROWS)
        y_att = _sparse_attention(proj, kn, vt, ikb, att_q_g[l], rel_bias, bsz, seq)
        kv = _matmul(_rmsnorm(memf, mem_g[l], t.NORM_ROWS), w_mem_kv[l],
                     tm=t.MEM_KV[0], tn=t.MEM_KV[1], name="mem_kv")
        y_mem = _memory_attention(proj, kv, mem_q_g[l], mem_k_g[l], bsz, seq, t.MEM_ATT_ROWS)
        merged = _merge(proj, y_ret, y_att, y_mem, w_up_ret[l], w_up_att[l], w_up_mem[l], d, t.MERGE_ROWS)
        xf = _matmul_ring(merged, w_out[l], tm=t.OUT_PROJ[0], tn=t.OUT_PROJ[1], residual=xf, name="out_proj")
        h2 = _rmsnorm(xf, ln2_g[l], t.NORM_ROWS)
        q = _matmul_ring(h2, peer_w_q[l], tm=t.PEER_QUERY[0], tn=t.PEER_QUERY[1], name="peer_query")
        c0t, e0t, r1t, e1t, u_bf, v_bf = _peer_router(q, peer_sub_keys[l], peer_u[l], peer_v[l],
                                                      t.ROUTER_TOKENS)
        p = _peer_gate(h2, u_bf, c0t, e0t, r1t, e1t, *t.GATE)
        xf = _peer_out(p, v_bf, xf, *t.PEER_OUT)
    return xf.reshape(bsz, seq, d)
```

```python
import functools
import math

import jax
import jax.numpy as jnp
from jax import lax
from jax.experimental import pallas as pl
from jax.experimental.pallas import tpu as pltpu

F32 = jnp.float32
BF16 = jnp.bfloat16
I32 = jnp.int32

MEM_LEN = 256
RET_HEADS, RET_DK, RET_DV, RET_CHUNK = 8, 128, 256, 128
ATT_HEADS, ATT_DH = 8, 128
IDX_HEADS, IDX_DH = 16, 64
TOPK_MAX = 256
MEM_HEADS, MEM_DH = 4, 256
REL_BUCKETS, REL_MAX_DIST = 32, 128
PEER_HEADS, PEER_NKEYS, PEER_DQ, PEER_TOPK = 8, 128, 256, 16
EPS = 1e-6
ROPE_BASE = 10000.0

OFF_RQ = 0
OFF_RK = OFF_RQ + RET_HEADS * RET_DK
OFF_RV = OFF_RK + RET_HEADS * RET_DK
OFF_RG = OFF_RV + RET_HEADS * RET_DV
OFF_AQ = OFF_RG + RET_HEADS * RET_DV
OFF_AK = OFF_AQ + ATT_HEADS * ATT_DH
OFF_AV = OFF_AK + ATT_DH
OFF_IQ = OFF_AV + ATT_DH
OFF_IK = OFF_IQ + IDX_HEADS * IDX_DH
OFF_IW = OFF_IK + IDX_DH
OFF_MQ = OFF_IW + IDX_HEADS
OFF_GATE = OFF_MQ + MEM_HEADS * MEM_DH

LANE = 128
GATE_SHIFT = OFF_GATE % LANE
GATE_BASE = OFF_GATE - GATE_SHIFT
MQ_SHIFT = OFF_MQ - OFF_IK
VMEM_LIMIT = 56 * 1024 * 1024


class Tiles:
    NORM_ROWS = 256
    IN_PROJ = (1024, 512)
    OUT_PROJ = (1024, 512)
    PEER_QUERY = (1024, 512)
    MEM_KV = (512, 512)
    KPREP_ROWS = 512
    MEM_ATT_ROWS = 512
    MERGE_ROWS = 2048
    ROUTER_TOKENS = 1024
    GATE = (512, 1024)
    PEER_OUT = (512, 2048, 2048)

INT_MIN = -(2 ** 31)
NEG_BIG = -1e30


def _cparams(sem):
    return pltpu.CompilerParams(dimension_semantics=sem, vmem_limit_bytes=VMEM_LIMIT)


def _dot_nt(a, b):
    return lax.dot_general(a, b, (((1,), (1,)), ((), ())), preferred_element_type=F32)


def _dot(a, b):
    return jnp.dot(a, b, preferred_element_type=F32)


def _rms(x):
    return x * lax.rsqrt(jnp.mean(x * x, axis=-1, keepdims=True) + EPS)


def _rmsnorm_kernel(x_ref, g_ref, o_ref):
    o_ref[...] = (_rms(x_ref[...].astype(F32)) * g_ref[...]).astype(o_ref.dtype)


def _rmsnorm(x, g, tm):
    m, d = x.shape
    tm = min(tm, m)
    return pl.pallas_call(
        _rmsnorm_kernel,
        out_shape=jax.ShapeDtypeStruct((m, d), BF16),
        grid=(m // tm,),
        in_specs=[pl.BlockSpec((tm, d), lambda i: (i, 0)),
                  pl.BlockSpec((1, d), lambda i: (0, 0))],
        out_specs=pl.BlockSpec((tm, d), lambda i: (i, 0)),
        compiler_params=_cparams(("arbitrary",)),
        name="rmsnorm",
    )(x, g.reshape(1, d))


def _mm_kernel(a_ref, w_ref, o_ref, wsc):
    @pl.when(pl.program_id(1) == 0)
    def _():
        wsc[...] = w_ref[...].astype(BF16)

    o_ref[...] = _dot(a_ref[...], wsc[...]).astype(o_ref.dtype)


def _matmul(a, w, *, tm, tn, name):
    m, k = a.shape
    n = w.shape[1]
    tm = min(tm, m)
    tn = min(tn, n)
    return pl.pallas_call(
        _mm_kernel,
        out_shape=jax.ShapeDtypeStruct((m, n), F32),
        grid=(pl.cdiv(n, tn), m // tm),
        in_specs=[pl.BlockSpec((tm, k), lambda j, i: (i, 0)),
                  pl.BlockSpec((k, tn), lambda j, i: (0, j))],
        out_specs=pl.BlockSpec((tm, tn), lambda j, i: (i, j)),
        scratch_shapes=[pltpu.VMEM((k, tn), BF16)],
        compiler_params=_cparams(("arbitrary", "arbitrary")),
        name=name,
    )(a, w)


A_RING = 3


def _mm_ring_kernel(*refs, has_res, w_transposed, tm, ni, nsteps):
    if has_res:
        a_hbm, w_ref, r_ref, o_ref, abuf, sem, wsc = refs
    else:
        a_hbm, w_ref, o_ref, abuf, sem, wsc = refs
    i = pl.program_id(1)
    s = pl.program_id(0) * ni + i

    def a_copy(step, slot):
        row = pl.multiple_of(lax.rem(step, ni) * tm, tm)
        return pltpu.make_async_copy(a_hbm.at[pl.ds(row, tm), :], abuf.at[slot], sem.at[slot])

    @pl.when(s == 0)
    def _():
        a_copy(0, 0).start()
        a_copy(1, 1).start()

    @pl.when(s + 2 < nsteps)
    def _():
        a_copy(s + 2, lax.rem(s + 2, A_RING)).start()

    @pl.when(i == 0)
    def _():
        wsc[...] = w_ref[...].astype(BF16)

    slot = lax.rem(s, A_RING)
    a_copy(s, slot).wait()
    acc = _dot_nt(abuf[slot], wsc[...]) if w_transposed else _dot(abuf[slot], wsc[...])
    if has_res:
        acc = acc + r_ref[...]
    o_ref[...] = acc.astype(o_ref.dtype)


def _matmul_ring(a, w, *, tm, tn, residual=None, out_dtype=F32, w_transposed=False, name="matmul_ring"):
    m, k = a.shape
    n = w.shape[0] if w_transposed else w.shape[1]
    tm = min(tm, m)
    tn = min(tn, n)
    ni = m // tm
    nj = pl.cdiv(n, tn)
    assert ni * nj >= 2
    w_block = (tn, k) if w_transposed else (k, tn)
    w_index = (lambda j, i: (j, 0)) if w_transposed else (lambda j, i: (0, j))
    in_specs = [pl.BlockSpec(memory_space=pl.ANY), pl.BlockSpec(w_block, w_index)]
    args = [a, w]
    if residual is not None:
        in_specs.append(pl.BlockSpec((tm, tn), lambda j, i: (i, j)))
        args.append(residual)
    return pl.pallas_call(
        functools.partial(_mm_ring_kernel, has_res=residual is not None, w_transposed=w_transposed,
                          tm=tm, ni=ni, nsteps=ni * nj),
        out_shape=jax.ShapeDtypeStruct((m, n), out_dtype),
        grid=(nj, ni),
        in_specs=in_specs,
        out_specs=pl.BlockSpec((tm, tn), lambda j, i: (i, j)),
        scratch_shapes=[pltpu.VMEM((A_RING, tm, k), BF16),
                        pltpu.SemaphoreType.DMA((A_RING,)),
                        pltpu.VMEM(w_block, BF16)],
        compiler_params=_cparams(("arbitrary", "arbitrary")),
        name=name,
    )(*args)


def _ret_kernel(cdec_ref, rq_ref, rk_ref, rv_ref, rg_ref, pos_ref, invf_ref, sign_ref,
                idec_ref, qdec_ref, kdec_ref, o_ref, st_ref):
    @pl.when(pl.program_id(1) == 0)
    def _():
        st_ref[...] = jnp.zeros_like(st_ref)

    ang = pos_ref[...].astype(F32) * invf_ref[...]
    cos = jnp.cos(ang)
    sin = jnp.sin(ang) * sign_ref[...]
    scale = RET_DK ** -0.5
    for h in range(RET_HEADS):
        q = rq_ref[:, h * RET_DK:(h + 1) * RET_DK]
        k = rk_ref[:, h * RET_DK:(h + 1) * RET_DK]
        q = q * cos + pltpu.roll(q, RET_DK // 2, 1) * sin
        k = (k * cos + pltpu.roll(k, RET_DK // 2, 1) * sin) * scale
        vb = rv_ref[:, h * RET_DV:(h + 1) * RET_DV].astype(BF16)
        s = _dot_nt(q.astype(BF16), k.astype(BF16)) * idec_ref[h]
        inner = _dot(s.astype(BF16), vb)
        st = st_ref[h]
        cross = _dot((q * qdec_ref[h]).astype(BF16), st.astype(BF16))
        kd = (k * kdec_ref[h]).astype(BF16)
        upd = lax.dot_general(kd, vb, (((0,), (0,)), ((), ())), preferred_element_type=F32)
        st_ref[h] = st * cdec_ref[h] + upd
        y = _rms(inner + cross)
        g = rg_ref[:, h * RET_DV:(h + 1) * RET_DV]
        o_ref[:, h * RET_DV:(h + 1) * RET_DV] = (g / (1.0 + jnp.exp(-g)) * y).astype(o_ref.dtype)


def _retention(proj, positions, bsz, seq):
    c = RET_CHUNK
    nc = seq // c
    m = bsz * seq
    hh = RET_HEADS
    log_gamma = jnp.log1p(-jnp.exp2(-5.0 - jnp.arange(hh, dtype=F32)))
    idx = jnp.arange(c, dtype=F32)
    diff = idx[:, None] - idx[None, :]
    idec = jnp.where(diff[None] >= 0,
                     jnp.exp(log_gamma[:, None, None] * jnp.maximum(diff, 0.0)[None]), 0.0)
    qdec = jnp.exp(log_gamma[:, None] * (idx + 1.0)[None])
    kdec = jnp.exp(log_gamma[:, None] * (c - 1.0 - idx)[None])
    cdec = jnp.exp(log_gamma * c)
    qdec_b = jnp.broadcast_to(qdec[:, :, None], (hh, c, RET_DK))
    kdec_b = jnp.broadcast_to(kdec[:, :, None], (hh, c, RET_DK))
    half = RET_DK // 2
    inv_freq = 1.0 / (ROPE_BASE ** (jnp.arange(half, dtype=F32) / half))
    invf = jnp.concatenate([inv_freq, inv_freq]).reshape(1, RET_DK)
    sign = jnp.concatenate([-jnp.ones((half,), F32), jnp.ones((half,), F32)]).reshape(1, RET_DK)
    qk_w = hh * RET_DK
    v_w = hh * RET_DV
    row = lambda b, i: b * nc + i
    const3 = lambda b, i: (0, 0, 0)
    return pl.pallas_call(
        _ret_kernel,
        out_shape=jax.ShapeDtypeStruct((m, v_w), BF16),
        grid=(bsz, nc),
        in_specs=[
            pl.BlockSpec(memory_space=pltpu.SMEM),
            pl.BlockSpec((c, qk_w), lambda b, i: (row(b, i), OFF_RQ // qk_w)),
            pl.BlockSpec((c, qk_w), lambda b, i: (row(b, i), OFF_RK // qk_w)),
            pl.BlockSpec((c, v_w), lambda b, i: (row(b, i), OFF_RV // v_w)),
            pl.BlockSpec((c, v_w), lambda b, i: (row(b, i), OFF_RG // v_w)),
            pl.BlockSpec((c, 1), lambda b, i: (row(b, i), 0)),
            pl.BlockSpec((1, RET_DK), lambda b, i: (0, 0)),
            pl.BlockSpec((1, RET_DK), lambda b, i: (0, 0)),
            pl.BlockSpec((hh, c, c), const3),
            pl.BlockSpec((hh, c, RET_DK), const3),
            pl.BlockSpec((hh, c, RET_DK), const3),
        ],
        out_specs=pl.BlockSpec((c, v_w), lambda b, i: (row(b, i), 0)),
        scratch_shapes=[pltpu.VMEM((hh, RET_DK, RET_DV), F32)],
        compiler_params=_cparams(("arbitrary", "arbitrary")),
        name="retention",
    )(cdec, proj, proj, proj, proj, positions.reshape(m, 1), invf, sign, idec, qdec_b, kdec_b)


def _kprep_kernel(ak_ref, av_ref, ikw_ref, kg_ref, kn_ref, vt_ref, ik_ref):
    kn_ref[...] = (_rms(ak_ref[...]) * kg_ref[...]).astype(BF16)
    vt_ref[...] = av_ref[...].T.astype(BF16)
    ik_ref[...] = ikw_ref[...].astype(BF16)


def _kprep(proj, att_k_g, tm):
    m = proj.shape[0]
    tm = min(tm, m)
    blk = lambda off: pl.BlockSpec((tm, LANE), lambda i: (i, off // LANE))
    nat = jax.ShapeDtypeStruct((m, LANE), BF16)
    nat_spec = pl.BlockSpec((tm, LANE), lambda i: (i, 0))
    return pl.pallas_call(
        _kprep_kernel,
        out_shape=(nat, jax.ShapeDtypeStruct((LANE, m), BF16), nat),
        grid=(m // tm,),
        in_specs=[blk(OFF_AK), blk(OFF_AV), blk(OFF_IK),
                  pl.BlockSpec((1, LANE), lambda i: (0, 0))],
        out_specs=(nat_spec, pl.BlockSpec((LANE, tm), lambda i: (0, i)), nat_spec),
        compiler_params=_cparams(("arbitrary",)),
        name="att_kprep",
    )(proj, proj, proj, att_k_g.reshape(1, ATT_DH))


IDX_TK = 512
FAR_TK = 512
M_INIT = -1e29


def _att_kernel(aq_ref, iq0_ref, iq1_ref, iq2_ref, iq3_ref, ikw_ref, kn_ref, vt_ref, ik_ref,
                qg_ref, bias_ref, o_ref, sc_ref, qt_ref, acc_ref, *, topk, tq, tki):
    qi = pl.program_id(1)
    t0 = qi * tq
    nkb = (t0 + tq + tki - 1) // tki
    col_t = t0 + lax.broadcasted_iota(I32, (1, tq), 1)
    sub = tki // LANE

    heads_t = []
    for ref in (iq0_ref, iq1_ref, iq2_ref, iq3_ref):
        for c in range(2):
            pair_t = ref[:, c * LANE:(c + 1) * LANE].T
            heads_t.append(pair_t[:IDX_DH])
            heads_t.append(pair_t[IDX_DH:])
    rhs_all = jnp.concatenate(heads_t, axis=1).astype(BF16)
    w_t = ikw_ref[...].T
    idx_scale = (IDX_DH ** -0.5) * (IDX_HEADS ** -0.5)

    def score_blk(kb, carry):
        k0 = pl.multiple_of(kb * tki, tki)
        ikb = ik_ref[pl.ds(k0, tki), :][:, :IDX_DH]
        lg = _dot(ikb, rhs_all)
        acc = None
        for h in range(IDX_HEADS):
            term = w_t[IDX_DH + h:IDX_DH + h + 1, :] * jnp.maximum(lg[:, h * tq:(h + 1) * tq], 0.0)
            acc = term if acc is None else acc + term
        acc = acc * idx_scale
        s_idx = k0 + lax.broadcasted_iota(I32, (tki, 1), 0)
        acc = jnp.where(s_idx <= col_t, acc, -jnp.inf)
        bits = pltpu.bitcast(acc, I32)
        sc_ref[pl.ds(k0, tki), :] = jnp.where(bits < 0, bits ^ 0x7FFFFFFF, bits)
        return carry

    lax.fori_loop(0, nkb, score_blk, 0)

    def count_ge(cand):
        def body(kb, cnt):
            k0 = pl.multiple_of(kb * tki, tki)
            c = jnp.where(sc_ref[pl.ds(k0, tki), :] >= cand, 1, 0)
            return cnt + jnp.sum(c.reshape(tki // 8, 8, tq), axis=0)
        cnt = lax.fori_loop(0, nkb, body, jnp.zeros((8, tq), I32))
        return jnp.sum(cnt, axis=0, keepdims=True)

    base = jnp.where(count_ge(jnp.zeros((1, tq), I32)) >= topk, 0, INT_MIN).astype(I32)

    def bit_body(i, base):
        cand = base | jnp.left_shift(jnp.int32(1), 30 - i)
        return jnp.where(count_ge(cand) >= topk, cand, base)

    tau = lax.fori_loop(0, 31, bit_body, base)

    qg = qg_ref[...]
    scale = ATT_DH ** -0.5
    tkf = min(FAR_TK, tki)
    fsub = tkf // LANE
    nfar = jnp.maximum(qi - 1, 0) // fsub

    for h in range(ATT_HEADS):
        qt_ref[h] = (_rms(aq_ref[:, h * ATT_DH:(h + 1) * ATT_DH]) * qg).T.astype(BF16)
    acc_ref[...] = jnp.zeros_like(acc_ref)

    def softmax_block(carry, k0, width, sel, bias_of=None, shift_of=None):
        m_all, l_all = carry
        kblk = kn_ref[pl.ds(k0, width), :]
        vblk = vt_ref[:, pl.ds(k0, width)]
        m_rows, l_rows = [], []
        for h in range(ATT_HEADS):
            lg = _dot(kblk, qt_ref[h]) * scale
            if bias_of is not None:
                lg = lg + bias_of(h)
            lg = jnp.where(sel, lg, NEG_BIG)
            m_i = m_all[h:h + 1]
            m_blk = jnp.max(lg, axis=0, keepdims=True)
            if shift_of is not None:
                m_blk = m_blk + shift_of(h)
            m_new = jnp.maximum(m_i, m_blk)
            alpha = jnp.exp(m_i - m_new)
            p = jnp.exp(lg - (m_new if shift_of is None else m_new - shift_of(h)))
            m_rows.append(m_new)
            l_rows.append(alpha * l_all[h:h + 1] + jnp.sum(p, axis=0, keepdims=True))
            acc_ref[h] = alpha * acc_ref[h] + _dot(vblk, p.astype(BF16))
        return jnp.concatenate(m_rows, axis=0), jnp.concatenate(l_rows, axis=0)

    def far_blk(kb, carry):
        k0 = pl.multiple_of(kb * tkf, tkf)
        sel = sc_ref[pl.ds(k0, tkf), :] >= tau
        return softmax_block(carry, k0, tkf, sel, shift_of=lambda h: bias_ref[2, h, 0:1, :])

    def near_blk(j, carry):
        k0 = pl.multiple_of(j * LANE, LANE)
        s_idx = k0 + lax.broadcasted_iota(I32, (LANE, 1), 0)
        sel = jnp.logical_and(sc_ref[pl.ds(k0, LANE), :] >= tau, s_idx <= col_t)
        d = jnp.minimum(qi - j, 2)
        return softmax_block(carry, k0, LANE, sel, bias_of=lambda h: bias_ref[d, h])

    init = (jnp.full((ATT_HEADS, tq), M_INIT, F32), jnp.zeros((ATT_HEADS, tq), F32))
    carry = lax.fori_loop(0, nfar, far_blk, init)
    _, l_f = lax.fori_loop(nfar * fsub, qi + 1, near_blk, carry)
    for h in range(ATT_HEADS):
        o_ref[:, h * ATT_DH:(h + 1) * ATT_DH] = (acc_ref[h] / l_f[h:h + 1]).T.astype(o_ref.dtype)


def _t5_bucket(dist):
    n = jnp.maximum(dist, 0)
    max_exact = REL_BUCKETS // 2
    large = max_exact + (jnp.log(jnp.maximum(n, 1).astype(F32) / max_exact)
                         / math.log(REL_MAX_DIST / max_exact) * (REL_BUCKETS - max_exact)).astype(I32)
    large = jnp.minimum(large, REL_BUCKETS - 1)
    return jnp.where(n < max_exact, n, large)


def _sparse_attention(proj, kn, vt, ikb, att_q_g, rel_bias, bsz, seq):
    tq = LANE
    m = bsz * seq
    nq = seq // tq
    topk = min(TOPK_MAX, seq // 4)
    tki = min(IDX_TK, seq)
    kk = jnp.arange(LANE, dtype=I32)[:, None]
    qq = jnp.arange(tq, dtype=I32)[None, :]
    dist = jnp.stack([d * LANE + qq - kk for d in range(3)])
    onehot = (_t5_bucket(dist)[..., None] == jnp.arange(REL_BUCKETS, dtype=I32)).astype(F32)
    bias = jnp.einsum("dkqb,bh->dhkq", onehot, rel_bias.astype(F32), precision=lax.Precision.HIGHEST)
    qw = ATT_HEADS * ATT_DH
    row = lambda b, i: b * nq + i
    iq_spec = lambda p: pl.BlockSpec((tq, 2 * LANE), lambda b, i: (row(b, i), OFF_IQ // (2 * LANE) + p))
    kspec = pl.BlockSpec((seq, LANE), lambda b, i: (b, 0))
    return pl.pallas_call(
        functools.partial(_att_kernel, topk=topk, tq=tq, tki=tki),
        out_shape=jax.ShapeDtypeStruct((m, qw), BF16),
        grid=(bsz, nq),
        in_specs=[
            pl.BlockSpec((tq, qw), lambda b, i: (row(b, i), OFF_AQ // qw)),
            iq_spec(0), iq_spec(1), iq_spec(2), iq_spec(3),
            pl.BlockSpec((tq, LANE), lambda b, i: (row(b, i), OFF_IK // LANE)),
            kspec,
            pl.BlockSpec((LANE, seq), lambda b, i: (0, b)),
            kspec,
            pl.BlockSpec((1, ATT_DH), lambda b, i: (0, 0)),
            pl.BlockSpec((3, ATT_HEADS, LANE, tq), lambda b, i: (0, 0, 0, 0)),
        ],
        out_specs=pl.BlockSpec((tq, qw), lambda b, i: (row(b, i), 0)),
        scratch_shapes=[pltpu.VMEM((seq, tq), I32),
                        pltpu.VMEM((ATT_HEADS, ATT_DH, tq), BF16),
                        pltpu.VMEM((ATT_HEADS, ATT_DH, tq), F32)],
        compiler_params=_cparams(("arbitrary", "arbitrary")),
        name="sparse_attention",
    )(proj, proj, proj, proj, proj, proj, kn, vt, ikb, att_q_g.reshape(1, ATT_DH), bias)


MQ_WIN = 3 * LANE


def _mem_kernel(w0_ref, w1_ref, w2_ref, kv_ref, qg_ref, kg_ref, o_ref):
    win = jnp.concatenate([w0_ref[...], w1_ref[...], w2_ref[...]], axis=1)
    mq = win[:, MQ_SHIFT:MQ_SHIFT + MEM_HEADS * MEM_DH]
    qg = qg_ref[...]
    kg = kg_ref[...]
    scale = MEM_DH ** -0.5
    kvw = MEM_HEADS * MEM_DH
    for h in range(MEM_HEADS):
        q = (_rms(mq[:, h * MEM_DH:(h + 1) * MEM_DH]) * qg).astype(BF16)
        k = (_rms(kv_ref[:, h * MEM_DH:(h + 1) * MEM_DH]) * kg).astype(BF16)
        v = kv_ref[:, kvw + h * MEM_DH:kvw + (h + 1) * MEM_DH].astype(BF16)
        lg = _dot_nt(q, k) * scale
        p = jnp.exp(lg - jnp.max(lg, axis=-1, keepdims=True))
        p = p / jnp.sum(p, axis=-1, keepdims=True)
        o_ref[:, h * MEM_DH:(h + 1) * MEM_DH] = _dot(p.astype(BF16), v).astype(o_ref.dtype)


def _memory_attention(proj, kv, mem_q_g, mem_k_g, bsz, seq, ts):
    ts = min(ts, seq)
    ns = seq // ts
    m = bsz * seq
    qw = MEM_HEADS * MEM_DH
    row = lambda b, i: b * ns + i
    wspec = lambda p: pl.BlockSpec((ts, MQ_WIN), lambda b, i: (row(b, i), OFF_IK // MQ_WIN + p))
    return pl.pallas_call(
        _mem_kernel,
        out_shape=jax.ShapeDtypeStruct((m, qw), BF16),
        grid=(bsz, ns),
        in_specs=[wspec(0), wspec(1), wspec(2),
                  pl.BlockSpec((MEM_LEN, 2 * qw), lambda b, i: (b, 0)),
                  pl.BlockSpec((1, MEM_DH), lambda b, i: (0, 0)),
                  pl.BlockSpec((1, MEM_DH), lambda b, i: (0, 0))],
        out_specs=pl.BlockSpec((ts, qw), lambda b, i: (row(b, i), 0)),
        compiler_params=_cparams(("arbitrary", "arbitrary")),
        name="memory_attention",
    )(proj, proj, proj, kv, mem_q_g.reshape(1, MEM_DH), mem_k_g.reshape(1, MEM_DH))


MERGE_TN = 256


def _merge_kernel(yr_ref, ya_ref, ym_ref, wr_ref, wa_ref, wm_ref,
                  g0_ref, e0_ref, g1_ref, e1_ref, g2_ref, e2_ref, o_ref):
    acc = None
    for y_ref, w_ref, g_ref, e_ref in ((yr_ref, wr_ref, g0_ref, e0_ref),
                                       (ya_ref, wa_ref, g1_ref, e1_ref),
                                       (ym_ref, wm_ref, g2_ref, e2_ref)):
        g = jnp.concatenate([g_ref[...], e_ref[...]], axis=1)[:, GATE_SHIFT:GATE_SHIFT + MERGE_TN]
        y = _dot(y_ref[...], w_ref[...].astype(BF16))
        term = y / (1.0 + jnp.exp(-g))
        acc = term if acc is None else acc + term
    o_ref[...] = acc.astype(o_ref.dtype)


def _merge(proj, y_ret, y_att, y_mem, w_up_ret, w_up_att, w_up_mem, d_model, tm):
    m = y_ret.shape[0]
    tm = min(tm, m)
    tn = MERGE_TN
    yspec = lambda y: pl.BlockSpec((tm, y.shape[1]), lambda i, n: (i, 0), pipeline_mode=pl.Buffered(1))
    wspec = lambda w: pl.BlockSpec((w.shape[0], tn), lambda i, n: (0, n))
    gspecs = []
    for br in range(3):
        base = GATE_BASE + br * d_model
        gspecs.append(pl.BlockSpec((tm, tn), lambda i, n, base=base: (i, base // tn + n)))
        gspecs.append(pl.BlockSpec((tm, LANE), lambda i, n, base=base: (i, base // LANE + (n + 1) * (tn // LANE))))
    return pl.pallas_call(
        _merge_kernel,
        out_shape=jax.ShapeDtypeStruct((m, d_model), BF16),
        grid=(m // tm, d_model // tn),
        in_specs=[yspec(y_ret), yspec(y_att), yspec(y_mem),
                  wspec(w_up_ret), wspec(w_up_att), wspec(w_up_mem)] + gspecs,
        out_specs=pl.BlockSpec((tm, tn), lambda i, n: (i, n)),
        compiler_params=_cparams(("arbitrary", "arbitrary")),
        name="gated_merge",
    )(y_ret, y_att, y_mem, w_up_ret, w_up_att, w_up_mem, *([proj] * 6))


RANK_NONE = 64.0


def _top_rows(vals, k, with_rank=False):
    rows = []
    rank = jnp.full(vals.shape, RANK_NONE, F32) if with_rank else None
    for r in range(k):
        mx = jnp.max(vals, axis=0, keepdims=True)
        rows.append(mx)
        top = vals == mx
        if with_rank:
            rank = jnp.where(top, r + 1.0, rank)
        vals = jnp.where(top, -jnp.inf, vals)
    return (rows, rank) if with_rank else rows


def _router_kernel(q_ref, keys_ref, u_ref, v_ref, c0_ref, e0_ref, r1_ref, e1_ref, ub_ref, vb_ref):
    ub_ref[...] = u_ref[...].astype(ub_ref.dtype)
    vb_ref[...] = v_ref[...].astype(vb_ref.dtype)
    kk = PEER_TOPK
    s_t = []
    for c in range(2):
        qs = q_ref[:, c * LANE:(c + 1) * LANE].astype(BF16)
        s_t.append(_dot_nt(keys_ref[c].astype(BF16), qs))
    a = _top_rows(s_t[0], kk)
    b, rank1 = _top_rows(s_t[1], kk, with_rank=True)
    amat = jnp.concatenate(a, axis=0)
    bmat = jnp.concatenate(b, axis=0)
    half = kk // 2
    cand = jnp.concatenate([a[0] + bmat] + [a[r] + bmat[:half] for r in range(1, half)]
                           + [amat[half:] + b[0]], axis=0)
    tv = _top_rows(cand, kk)
    tau = tv[kk - 1]
    z = None
    for r in range(kk):
        e = jnp.exp(tv[r] - tv[0])
        z = e if z is None else z + e
    cnt = jnp.zeros(amat.shape, F32)
    for j in range(kk):
        cnt = cnt + jnp.where(amat + b[j] >= tau, 1.0, 0.0)
    count0 = jnp.zeros(s_t[0].shape, F32)
    for r in range(kk):
        count0 = jnp.where(s_t[0] == a[r], cnt[r:r + 1], count0)
    c0_ref[0] = count0
    e0_ref[0] = jnp.exp(s_t[0] - a[0]) * (1.0 / z)
    r1_ref[0] = rank1
    e1_ref[0] = jnp.exp(s_t[1] - b[0])


def _peer_router(q, sub_keys, u_tab, v_tab, tt):
    m = q.shape[0]
    tt = min(tt, m)
    nk = PEER_NKEYS
    ne, d = u_tab.shape
    steps = (m // tt) * PEER_HEADS
    slab = ne // steps
    assert slab * steps == ne and slab % 16 == 0
    keys = sub_keys.reshape(PEER_HEADS * 2, nk, PEER_DQ // 2)
    f_shape = jax.ShapeDtypeStruct((PEER_HEADS, nk, m), F32)
    t_shape = jax.ShapeDtypeStruct((ne, d), BF16)
    s_spec = pl.BlockSpec((1, nk, tt), lambda i, h: (h, 0, i))
    t_spec = pl.BlockSpec((slab, d), lambda i, h: (i * PEER_HEADS + h, 0))
    return pl.pallas_call(
        _router_kernel,
        out_shape=(f_shape,) * 4 + (t_shape,) * 2,
        grid=(m // tt, PEER_HEADS),
        in_specs=[pl.BlockSpec((tt, PEER_DQ), lambda i, h: (i, h)),
                  pl.BlockSpec((2, nk, PEER_DQ // 2), lambda i, h: (h, 0, 0)),
                  t_spec, t_spec],
        out_specs=(s_spec,) * 4 + (t_spec,) * 2,
        compiler_params=_cparams(("arbitrary", "arbitrary")),
        name="peer_router",
    )(q, keys, u_tab, v_tab)


def _gelu(x):
    return 0.5 * x * (1.0 + lax.erf(x * (2.0 ** -0.5)))


def _peer_gate_kernel(h_ref, u_ref, c0_ref, e0_ref, r1_ref, e1_ref, o_ref, z_ref, *, te):
    e = pl.program_id(1)
    nsub = te // PEER_NKEYS
    tm = h_ref.shape[0]
    z_ref[...] = _dot_nt(u_ref[...], h_ref[...])
    for ii in range(nsub):
        i_idx = e * nsub + ii
        rows = slice(ii * PEER_NKEYS, (ii + 1) * PEER_NKEYS)
        c0_rows = [c0_ref[h, pl.ds(i_idx, 1), :] for h in range(PEER_HEADS)]
        e0_rows = [e0_ref[h, pl.ds(i_idx, 1), :] for h in range(PEER_HEADS)]
        for tc in range(tm // LANE):
            cols = slice(tc * LANE, (tc + 1) * LANE)
            w = None
            for h in range(PEER_HEADS):
                hit = r1_ref[h, :, cols] <= c0_rows[h][:, cols]
                term = jnp.where(hit, e1_ref[h, :, cols] * e0_rows[h][:, cols], 0.0)
                w = term if w is None else w + term
            p_t = _gelu(z_ref[rows, cols]) * w
            o_ref[rows, cols] = p_t.astype(o_ref.dtype)


def _peer_gate(h2, u_bf, c0t, e0t, r1t, e1t, tm, te):
    m, d = h2.shape
    ne = u_bf.shape[0]
    tm = min(tm, m)
    sspec = pl.BlockSpec((PEER_HEADS, PEER_NKEYS, tm), lambda i, e: (0, 0, i))
    return pl.pallas_call(
        functools.partial(_peer_gate_kernel, te=te),
        out_shape=jax.ShapeDtypeStruct((ne, m), BF16),
        grid=(m // tm, ne // te),
        in_specs=[pl.BlockSpec((tm, d), lambda i, e: (i, 0)),
                  pl.BlockSpec((te, d), lambda i, e: (e, 0)),
                  sspec, sspec, sspec, sspec],
        out_specs=pl.BlockSpec((te, tm), lambda i, e: (e, i)),
        scratch_shapes=[pltpu.VMEM((te, tm), F32)],
        compiler_params=_cparams(("arbitrary", "arbitrary")),
        name="peer_gate",
    )(h2, u_bf, c0t, e0t, r1t, e1t)


def _peer_out_kernel(p_ref, v_ref, x_ref, o_ref):
    @pl.when(pl.program_id(2) == 0)
    def _():
        o_ref[...] = x_ref[...]

    o_ref[...] += lax.dot_general(p_ref[...], v_ref[...], (((0,), (0,)), ((), ())),
                                  preferred_element_type=F32)


def _peer_out(p_t, v_bf, x, tm, tn, tk):
    ne, m = p_t.shape
    d = v_bf.shape[1]
    tm, tn, tk = min(tm, m), min(tn, d), min(tk, ne)
    return pl.pallas_call(
        _peer_out_kernel,
        out_shape=jax.ShapeDtypeStruct((m, d), F32),
        grid=(m // tm, d // tn, ne // tk),
        in_specs=[pl.BlockSpec((tk, tm), lambda i, n, k: (k, i)),
                  pl.BlockSpec((tk, tn), lambda i, n, k: (k, n)),
                  pl.BlockSpec((tm, tn), lambda i, n, k: (i, n))],
        out_specs=pl.BlockSpec((tm, tn), lambda i, n, k: (i, n)),
        compiler_params=_cparams(("arbitrary", "arbitrary", "arbitrary")),
        name="peer_out",
    )(p_t, v_bf, x)


def kernel(x, mem, positions, ln1_g, w_in, att_q_g, att_k_g, rel_bias, mem_g, w_mem_kv, mem_q_g, mem_k_g,
           w_up_ret, w_up_att, w_up_mem, w_out, ln2_g, peer_w_q, peer_sub_keys, peer_u, peer_v):
    bsz, seq, d = x.shape
    m = bsz * seq
    depth = ln1_g.shape[0]
    xf = x.reshape(m, d)
    memf = mem.reshape(bsz * mem.shape[1], d)
    t = Tiles
    for l in range(depth):
        h = _rmsnorm(xf, ln1_g[l], t.NORM_ROWS)
        proj = _matmul_ring(h, jnp.swapaxes(w_in[l], 0, 1), tm=t.IN_PROJ[0], tn=t.IN_PROJ[1],
                            w_transposed=True, name="in_proj")
        y_ret = _retention(proj, positions, bsz, seq)
        kn, vt, ikb = _kprep(proj, att_k_g[l], t.KPREP_ROWS)
        y_att = _sparse_attention(proj, kn, vt, ikb, att_q_g[l], rel_bias, bsz, seq)
        kv = _matmul(_rmsnorm(memf, mem_g[l], t.NORM_ROWS), w_mem_kv[l],
                     tm=t.MEM_KV[0], tn=t.MEM_KV[1], name="mem_kv")
        y_mem = _memory_attention(proj, kv, mem_q_g[l], mem_k_g[l], bsz, seq, t.MEM_ATT_ROWS)
        merged = _merge(proj, y_ret, y_att, y_mem, w_up_ret[l], w_up_att[l], w_up_mem[l], d, t.MERGE_ROWS)
        xf = _matmul_ring(merged, w_out[l], tm=t.OUT_PROJ[0], tn=t.OUT_PROJ[1], residual=xf, name="out_proj")
        h2 = _rmsnorm(xf, ln2_g[l], t.NORM_ROWS)
        q = _matmul_ring(h2, peer_w_q[l], tm=t.PEER_QUERY[0], tn=t.PEER_QUERY[1], name="peer_query")
        c0t, e0t, r1t, e1t, u_bf, v_bf = _peer_router(q, peer_sub_keys[l], peer_u[l], peer_v[l],
                                                      t.ROUTER_TOKENS)
        p = _peer_gate(h2, u_bf, c0t, e0t, r1t, e1t, *t.GATE)
        xf = _peer_out(p, v_bf, xf, *t.PEER_OUT)
    return xf.reshape(bsz, seq, d)
```
